```python
import math
import jax, jax.numpy as jnp
from jax import lax
import numpy as np

D_MODEL = 1024
BATCH = 16
SEQ = 2048
DEPTH = 2
DEC_BATCH = 32
DEC_SEQ = 8
PAST_LEN = 16384
PAGE_SIZE = 128

N_A_LAYERS = DEPTH // 2
N_B_LAYERS = DEPTH - DEPTH // 2
GROUP_CH = 16
N_GROUPS = D_MODEL // GROUP_CH
STATE_N = 64
N_HEADS = 16
HEAD_DIM = D_MODEL // N_HEADS
D_FF = 2816
Q_BLOCK = 128
SCAN_CHUNK = 128
N_MOD = 9
EPS = 1e-6
DT_MIN = 1e-3
DT_MAX = 1e-1
SB_BIAS_HI = -3.0
SB_BIAS_LO = -11.0

kernel_name = "yoco_s5_stickbreak_decode_step"


def _rmsnorm(x, g):
    xf = x.astype(jnp.float32)
    xf = xf * lax.rsqrt(jnp.mean(xf * xf, axis=-1, keepdims=True) + EPS)
    return xf.astype(x.dtype) * g


def _modulate(h, shift, scale):
    return h * (1 + scale[:, None, :]) + shift[:, None, :]


def _swiglu(h, w_gate, w_up, w_down):
    return (jax.nn.silu(h @ w_gate) * (h @ w_up)) @ w_down


def _ssm_combine(left, right):
    a_l, b_l = left
    a_r, b_r = right
    return a_r * a_l, a_r * b_l + b_r


def _s5_ssm(u, h0_re, h0_im, a_re, a_im, log_dt, b_re, b_im, c_re, c_im, d_skip):
    bsz, t_len, _ = u.shape
    f32 = jnp.float32
    uf = u.astype(f32)
    lam = lax.complex(a_re.astype(f32), a_im.astype(f32))
    dt = jnp.exp(log_dt.astype(f32))[:, None]
    a_bar = jnp.exp(lam * dt)
    b_bar = ((a_bar - 1) / lam)[:, :, None] * lax.complex(b_re.astype(f32), b_im.astype(f32))
    c_mat = lax.complex(c_re.astype(f32), c_im.astype(f32))
    chunk = SCAN_CHUNK if t_len % SCAN_CHUNK == 0 else t_len
    n_chunks = t_len // chunk
    u_chunks = uf.reshape(bsz, n_chunks, chunk, N_GROUPS, GROUP_CH).transpose(1, 0, 2, 3, 4)

    def step(h, u_c):
        bu = jnp.einsum("gnc,btgc->btgn", b_bar, u_c.astype(jnp.complex64))
        a = jnp.broadcast_to(a_bar, bu.shape)
        a_cum, hs = lax.associative_scan(_ssm_combine, (a, bu), axis=1)
        hs = hs + a_cum * h[:, None]
        y_c = jnp.real(jnp.einsum("gcn,btgn->btgc", c_mat, hs))
        return hs[:, -1], y_c

    h0 = lax.complex(h0_re.astype(f32), h0_im.astype(f32))
    h_last, ys = lax.scan(step, h0, u_chunks)
    y = ys.transpose(1, 0, 2, 3, 4).reshape(bsz, t_len, D_MODEL) + d_skip.astype(f32) * uf
    return y, jnp.real(h_last), jnp.imag(h_last)


def _stick_breaking(q, segs, q_offset, bias):
    tq = q.shape[1]
    f32 = jnp.float32
    bias_f = bias.astype(f32)[None, :, None, None]
    outs = []
    for s in range(0, tq, Q_BLOCK):
        e = min(s + Q_BLOCK, tq)
        n_keys = q_offset + e - 1
        qb = q[:, s:e].astype(f32) * HEAD_DIM ** -0.5
        zs, vs = [], []
        off = 0
        for k, v in segs:
            n = min(k.shape[1], n_keys - off)
            if n > 0:
                zs.append(jnp.einsum("bqhd,bkhd->bhqk", qb, k[:, :n].astype(f32)))
                vs.append(v[:, :n])
            off += k.shape[1]
        z = (jnp.concatenate(zs, axis=-1) if len(zs) > 1 else zs[0]) + bias_f
        q_pos = q_offset + s + jnp.arange(e - s)
        k_pos = jnp.arange(z.shape[-1])
        mask = k_pos[None, :] < q_pos[:, None]
        log_1m = jnp.where(mask, jax.nn.log_sigmoid(-z), 0.0)
        suffix = lax.cumsum(log_1m, axis=3, reverse=True) - log_1m
        w = jnp.where(mask, jnp.exp(jax.nn.log_sigmoid(z) + suffix), 0.0)
        o = 0.0
        start = 0
        for v in vs:
            n = v.shape[1]
            o = o + jnp.einsum("bhqk,bkhd->bqhd", w[..., start:start + n], v.astype(f32))
            start += n
        outs.append(o)
    return jnp.concatenate(outs, axis=1).astype(q.dtype)


def _trunk(x, c, h0_re, h0_im, past_k, past_v, ada_w, ada_b, norm_g, ffn_w_gate, ffn_w_up,
           ffn_w_down, ssm_a_re, ssm_a_im, ssm_log_dt, ssm_b_re, ssm_b_im, ssm_c_re, ssm_c_im,
           ssm_d, glu_w, glu_b, kv_ada_w, kv_ada_b, kv_norm_g, w_kv, w_q, w_o, sb_bias, norm_f):
    bsz, t_new, _ = x.shape
    sc = jax.nn.silu(c)
    q_offset = 0 if past_k is None else past_k.shape[1]
    new_re, new_im = [], []
    k_new = v_new = None
    segs = None
    for l in range(DEPTH):
        if l == N_A_LAYERS:
            kv_mod = (sc @ kv_ada_w + kv_ada_b).reshape(bsz, 2, D_MODEL)
            hk = _modulate(_rmsnorm(x, kv_norm_g), kv_mod[:, 0], kv_mod[:, 1])
            kv = hk @ w_kv
            k_new = kv[..., :D_MODEL].reshape(bsz, t_new, N_HEADS, HEAD_DIM)
            v_new = kv[..., D_MODEL:].reshape(bsz, t_new, N_HEADS, HEAD_DIM)
            segs = [(k_new, v_new)] if past_k is None else [(past_k, past_v), (k_new, v_new)]
        mod = (sc @ ada_w[l] + ada_b[l]).reshape(bsz, N_MOD, D_MODEL)
        h = _modulate(_rmsnorm(x, norm_g[l, 0]), mod[:, 0], mod[:, 1])
        x = x + 0.5 * mod[:, 2][:, None] * _swiglu(h, ffn_w_gate[l, 0], ffn_w_up[l, 0], ffn_w_down[l, 0])
        h = _modulate(_rmsnorm(x, norm_g[l, 1]), mod[:, 3], mod[:, 4])
        if l < N_A_LAYERS:
            y, s_re, s_im = _s5_ssm(h, h0_re[l], h0_im[l], ssm_a_re[l], ssm_a_im[l], ssm_log_dt[l],
                                    ssm_b_re[l], ssm_b_im[l], ssm_c_re[l], ssm_c_im[l], ssm_d[l])
            y = jax.nn.gelu(y).astype(x.dtype)
            gl = y @ glu_w[l] + glu_b[l]
            mix = gl[..., :D_MODEL] * jax.nn.sigmoid(gl[..., D_MODEL:])
            new_re.append(s_re)
            new_im.append(s_im)
        else:
            j = l - N_A_LAYERS
            q = (h @ w_q[j]).reshape(bsz, t_new, N_HEADS, HEAD_DIM)
            o = _stick_breaking(q, segs, q_offset, sb_bias[j])
            mix = o.reshape(bsz, t_new, D_MODEL) @ w_o[j]
        x = x + mod[:, 5][:, None] * mix
        h = _modulate(_rmsnorm(x, norm_g[l, 2]), mod[:, 6], mod[:, 7])
        x = x + 0.5 * mod[:, 8][:, None] * _swiglu(h, ffn_w_gate[l, 1], ffn_w_up[l, 1], ffn_w_down[l, 1])
    y = _rmsnorm(x, norm_f)
    return y, k_new, v_new, jnp.stack(new_re), jnp.stack(new_im)


def setup_inputs(seed: int = 0) -> dict:
    key = jax.random.key(seed)
    ks = jax.random.split(key, 40)
    f32 = jnp.float32
    D = D_MODEL
    n_pages = PAST_LEN // PAGE_SIZE
    n_pool = (DEC_BATCH * n_pages * 5) // 4

    def nrm(k, shape, scale):
        return jax.random.normal(k, shape, f32) * scale

    perm = jax.random.permutation(ks[6], n_pool)
    page_table = perm[: DEC_BATCH * n_pages].reshape(DEC_BATCH, n_pages).astype(jnp.int32)
    a_im = jnp.pi * jnp.arange(STATE_N, dtype=f32) + nrm(ks[14], (N_A_LAYERS, N_GROUPS, STATE_N), 0.01)
    sb_bias = jnp.linspace(SB_BIAS_HI, SB_BIAS_LO, N_HEADS, dtype=f32)[None] + nrm(ks[32], (N_B_LAYERS, N_HEADS), 0.1)
    return {
        "x_prompt": nrm(ks[0], (BATCH, SEQ, D), 1.0),
        "x_sample": nrm(ks[1], (DEC_BATCH, DEC_SEQ, D), 1.0),
        "cache_k": nrm(ks[2], (n_pool, PAGE_SIZE, N_HEADS, HEAD_DIM), 1.0),
        "cache_v": nrm(ks[3], (n_pool, PAGE_SIZE, N_HEADS, HEAD_DIM), 1.0),
        "state_ssm_re": nrm(ks[4], (N_A_LAYERS, DEC_BATCH, N_GROUPS, STATE_N), 0.1),
        "state_ssm_im": nrm(ks[5], (N_A_LAYERS, DEC_BATCH, N_GROUPS, STATE_N), 0.1),
        "page_table": page_table,
        "c_prompt": nrm(ks[7], (BATCH, D), 1.0),
        "c_sample": nrm(ks[8], (DEC_BATCH, D), 1.0),
        "ada_w": nrm(ks[9], (DEPTH, D, N_MOD * D), 0.5 * D ** -0.5),
        "ada_b": nrm(ks[10], (DEPTH, N_MOD * D), 0.02),
        "norm_g": 1.0 + nrm(ks[11], (DEPTH, 3, D), 0.02),
        "ffn_w_gate": nrm(ks[12], (DEPTH, 2, D, D_FF), D ** -0.5),
        "ffn_w_up": nrm(ks[13], (DEPTH, 2, D, D_FF), D ** -0.5),
        "ffn_w_down": nrm(ks[15], (DEPTH, 2, D_FF, D), D_FF ** -0.5),
        "ssm_a_re": -0.5 + nrm(ks[16], (N_A_LAYERS, N_GROUPS, STATE_N), 0.01),
        "ssm_a_im": a_im,
        "ssm_log_dt": jax.random.uniform(ks[17], (N_A_LAYERS, N_GROUPS), f32,
                                         math.log(DT_MIN), math.log(DT_MAX)),
        "ssm_b_re": nrm(ks[18], (N_A_LAYERS, N_GROUPS, STATE_N, GROUP_CH), (2 * GROUP_CH) ** -0.5),
        "ssm_b_im": nrm(ks[19], (N_A_LAYERS, N_GROUPS, STATE_N, GROUP_CH), (2 * GROUP_CH) ** -0.5),
        "ssm_c_re": nrm(ks[20], (N_A_LAYERS, N_GROUPS, GROUP_CH, STATE_N), STATE_N ** -0.5),
        "ssm_c_im": nrm(ks[21], (N_A_LAYERS, N_GROUPS, GROUP_CH, STATE_N), STATE_N ** -0.5),
        "ssm_d": nrm(ks[22], (N_A_LAYERS, D), 1.0),
        "glu_w": nrm(ks[23], (N_A_LAYERS, D, 2 * D), D ** -0.5),
        "glu_b": nrm(ks[24], (N_A_LAYERS, 2 * D), 0.02),
        "kv_ada_w": nrm(ks[25], (D, 2 * D), 0.5 * D ** -0.5),
        "kv_ada_b": nrm(ks[26], (2 * D,), 0.02),
        "kv_norm_g": 1.0 + nrm(ks[27], (D,), 0.02),
        "w_kv": nrm(ks[28], (D, 2 * D), D ** -0.5),
        "w_q": nrm(ks[29], (N_B_LAYERS, D, D), D ** -0.5),
        "w_o": nrm(ks[30], (N_B_LAYERS, D, D), D ** -0.5),
        "sb_bias": sb_bias,
        "norm_f": 1.0 + nrm(ks[31], (D,), 0.02),
    }


def reference(x_prompt, x_sample, cache_k, cache_v, state_ssm_re, state_ssm_im, page_table,
              c_prompt, c_sample, ada_w, ada_b, norm_g, ffn_w_gate, ffn_w_up, ffn_w_down,
              ssm_a_re, ssm_a_im, ssm_log_dt, ssm_b_re, ssm_b_im, ssm_c_re, ssm_c_im, ssm_d,
              glu_w, glu_b, kv_ada_w, kv_ada_b, kv_norm_g, w_kv, w_q, w_o, sb_bias, norm_f):
    weights = (ada_w, ada_b, norm_g, ffn_w_gate, ffn_w_up, ffn_w_down, ssm_a_re, ssm_a_im,
               ssm_log_dt, ssm_b_re, ssm_b_im, ssm_c_re, ssm_c_im, ssm_d, glu_w, glu_b,
               kv_ada_w, kv_ada_b, kv_norm_g, w_kv, w_q, w_o, sb_bias, norm_f)
    h0 = jnp.zeros((N_A_LAYERS, x_prompt.shape[0], N_GROUPS, STATE_N), jnp.float32)
    y_prompt, k_prompt, v_prompt, ssm_re_prompt, ssm_im_prompt = _trunk(
        x_prompt, c_prompt, h0, h0, None, None, *weights)
    n_seq, n_pages = page_table.shape
    past_k = cache_k[page_table].reshape(n_seq, n_pages * PAGE_SIZE, N_HEADS, HEAD_DIM)
    past_v = cache_v[page_table].reshape(n_seq, n_pages * PAGE_SIZE, N_HEADS, HEAD_DIM)
    y_sample, k_sample, v_sample, ssm_re_sample, ssm_im_sample = _trunk(
        x_sample, c_sample, state_ssm_re, state_ssm_im, past_k, past_v, *weights)
    return (y_prompt, y_sample, k_prompt, v_prompt, k_sample, v_sample,
            ssm_re_prompt, ssm_im_prompt, ssm_re_sample, ssm_im_sample)
```

```python
import math
import jax, jax.numpy as jnp
from jax import lax
import numpy as np
from jax.experimental import pallas as pl
from jax.experimental.pallas import tpu as pltpu

D_MODEL = 1024
BATCH = 16
SEQ = 2048
DEPTH = 2
DEC_BATCH = 32
DEC_SEQ = 8
PAST_LEN = 16384
PAGE_SIZE = 128

N_A_LAYERS = DEPTH // 2
N_B_LAYERS = DEPTH - DEPTH // 2
GROUP_CH = 16
N_GROUPS = D_MODEL // GROUP_CH
STATE_N = 64
N_HEADS = 16
HEAD_DIM = D_MODEL // N_HEADS
D_FF = 2816
Q_BLOCK = 128
SCAN_CHUNK = 128
N_MOD = 9
EPS = 1e-6


def _rmsnorm(x, g):
    xf = x.astype(jnp.float32)
    xf = xf * lax.rsqrt(jnp.mean(xf * xf, axis=-1, keepdims=True) + EPS)
    return xf.astype(x.dtype) * g


def _modulate(h, shift, scale):
    return h * (1 + scale[:, None, :]) + shift[:, None, :]


def _swiglu(h, w_gate, w_up, w_down):
    return (jax.nn.silu(h @ w_gate) * (h @ w_up)) @ w_down


def _ssm_combine(left, right):
    a_l, b_l = left
    a_r, b_r = right
    return a_r * a_l, a_r * b_l + b_r


def _s5_ssm(u, h0_re, h0_im, a_re, a_im, log_dt, b_re, b_im, c_re, c_im, d_skip):
    bsz, t_len, _ = u.shape
    f32 = jnp.float32
    uf = u.astype(f32)
    lam = lax.complex(a_re.astype(f32), a_im.astype(f32))
    dt = jnp.exp(log_dt.astype(f32))[:, None]
    a_bar = jnp.exp(lam * dt)
    b_bar = ((a_bar - 1) / lam)[:, :, None] * lax.complex(b_re.astype(f32), b_im.astype(f32))
    c_mat = lax.complex(c_re.astype(f32), c_im.astype(f32))
    chunk = SCAN_CHUNK if t_len % SCAN_CHUNK == 0 else t_len
    n_chunks = t_len // chunk
    u_chunks = uf.reshape(bsz, n_chunks, chunk, N_GROUPS, GROUP_CH).transpose(1, 0, 2, 3, 4)

    def step(h, u_c):
        bu = jnp.einsum("gnc,btgc->btgn", b_bar, u_c.astype(jnp.complex64))
        a = jnp.broadcast_to(a_bar, bu.shape)
        a_cum, hs = lax.associative_scan(_ssm_combine, (a, bu), axis=1)
        hs = hs + a_cum * h[:, None]
        y_c = jnp.real(jnp.einsum("gcn,btgn->btgc", c_mat, hs))
        return hs[:, -1], y_c

    h0 = lax.complex(h0_re.astype(f32), h0_im.astype(f32))
    h_last, ys = lax.scan(step, h0, u_chunks)
    y = ys.transpose(1, 0, 2, 3, 4).reshape(bsz, t_len, D_MODEL) + d_skip.astype(f32) * uf
    return y, jnp.real(h_last), jnp.imag(h_last)


def _stick_breaking(q, segs, q_offset, bias):
    tq = q.shape[1]
    f32 = jnp.float32
    bias_f = bias.astype(f32)[None, :, None, None]
    outs = []
    for s in range(0, tq, Q_BLOCK):
        e = min(s + Q_BLOCK, tq)
        n_keys = q_offset + e - 1
        qb = q[:, s:e].astype(f32) * HEAD_DIM ** -0.5
        zs, vs = [], []
        off = 0
        for k, v in segs:
            n = min(k.shape[1], n_keys - off)
            if n > 0:
                zs.append(jnp.einsum("bqhd,bkhd->bhqk", qb, k[:, :n].astype(f32)))
                vs.append(v[:, :n])
            off += k.shape[1]
        z = (jnp.concatenate(zs, axis=-1) if len(zs) > 1 else zs[0]) + bias_f
        q_pos = q_offset + s + jnp.arange(e - s)
        k_pos = jnp.arange(z.shape[-1])
        mask = k_pos[None, :] < q_pos[:, None]
        log_1m = jnp.where(mask, jax.nn.log_sigmoid(-z), 0.0)
        suffix = lax.cumsum(log_1m, axis=3, reverse=True) - log_1m
        w = jnp.where(mask, jnp.exp(jax.nn.log_sigmoid(z) + suffix), 0.0)
        o = 0.0
        start = 0
        for v in vs:
            n = v.shape[1]
            o = o + jnp.einsum("bhqk,bkhd->bqhd", w[..., start:start + n], v.astype(f32))
            start += n
        outs.append(o)
    return jnp.concatenate(outs, axis=1).astype(q.dtype)


def _trunk(x, c, h0_re, h0_im, past_k, past_v, ada_w, ada_b, norm_g, ffn_w_gate, ffn_w_up,
           ffn_w_down, ssm_a_re, ssm_a_im, ssm_log_dt, ssm_b_re, ssm_b_im, ssm_c_re, ssm_c_im,
           ssm_d, glu_w, glu_b, kv_ada_w, kv_ada_b, kv_norm_g, w_kv, w_q, w_o, sb_bias, norm_f):
    bsz, t_new, _ = x.shape
    sc = jax.nn.silu(c)
    q_offset = 0 if past_k is None else past_k.shape[1]
    new_re, new_im = [], []
    k_new = v_new = None
    segs = None
    for l in range(DEPTH):
        if l == N_A_LAYERS:
            kv_mod = (sc @ kv_ada_w + kv_ada_b).reshape(bsz, 2, D_MODEL)
            hk = _modulate(_rmsnorm(x, kv_norm_g), kv_mod[:, 0], kv_mod[:, 1])
            kv = hk @ w_kv
            k_new = kv[..., :D_MODEL].reshape(bsz, t_new, N_HEADS, HEAD_DIM)
            v_new = kv[..., D_MODEL:].reshape(bsz, t_new, N_HEADS, HEAD_DIM)
            segs = [(k_new, v_new)] if past_k is None else [(past_k, past_v), (k_new, v_new)]
        mod = (sc @ ada_w[l] + ada_b[l]).reshape(bsz, N_MOD, D_MODEL)
        h = _modulate(_rmsnorm(x, norm_g[l, 0]), mod[:, 0], mod[:, 1])
        x = x + 0.5 * mod[:, 2][:, None] * _swiglu(h, ffn_w_gate[l, 0], ffn_w_up[l, 0], ffn_w_down[l, 0])
        h = _modulate(_rmsnorm(x, norm_g[l, 1]), mod[:, 3], mod[:, 4])
        if l < N_A_LAYERS:
            y, s_re, s_im = _s5_ssm(h, h0_re[l], h0_im[l], ssm_a_re[l], ssm_a_im[l], ssm_log_dt[l],
                                    ssm_b_re[l], ssm_b_im[l], ssm_c_re[l], ssm_c_im[l], ssm_d[l])
            y = jax.nn.gelu(y).astype(x.dtype)
            gl = y @ glu_w[l] + glu_b[l]
            mix = gl[..., :D_MODEL] * jax.nn.sigmoid(gl[..., D_MODEL:])
            new_re.append(s_re)
            new_im.append(s_im)
        else:
            j = l - N_A_LAYERS
            q = (h @ w_q[j]).reshape(bsz, t_new, N_HEADS, HEAD_DIM)
            o = _stick_breaking(q, segs, q_offset, sb_bias[j])
            mix = o.reshape(bsz, t_new, D_MODEL) @ w_o[j]
        x = x + mod[:, 5][:, None] * mix
        h = _modulate(_rmsnorm(x, norm_g[l, 2]), mod[:, 6], mod[:, 7])
        x = x + 0.5 * mod[:, 8][:, None] * _swiglu(h, ffn_w_gate[l, 1], ffn_w_up[l, 1], ffn_w_down[l, 1])
    y = _final_norm(x, norm_f)
    return y, k_new, v_new, jnp.stack(new_re), jnp.stack(new_im)


def _final_norm_kernel(x_ref, g_ref, o_ref):
    x = x_ref[...]
    o_ref[...] = x * lax.rsqrt(jnp.mean(x * x, axis=-1, keepdims=True) + EPS) * g_ref[...]


def _final_norm(x, g):
    b, t, d = x.shape
    x2 = x.reshape(b * t, d)
    tm = 256
    out = pl.pallas_call(
        _final_norm_kernel,
        grid=(b * t // tm,),
        in_specs=[pl.BlockSpec((tm, d), lambda i: (i, 0)), pl.BlockSpec((1, d), lambda i: (0, 0))],
        out_specs=pl.BlockSpec((tm, d), lambda i: (i, 0)),
        out_shape=jax.ShapeDtypeStruct((b * t, d), x.dtype),
    )(x2, g.reshape(1, d))
    return out.reshape(b, t, d)


def kernel(x_prompt, x_sample, cache_k, cache_v, state_ssm_re, state_ssm_im, page_table,
           c_prompt, c_sample, ada_w, ada_b, norm_g, ffn_w_gate, ffn_w_up, ffn_w_down,
           ssm_a_re, ssm_a_im, ssm_log_dt, ssm_b_re, ssm_b_im, ssm_c_re, ssm_c_im, ssm_d,
           glu_w, glu_b, kv_ada_w, kv_ada_b, kv_norm_g, w_kv, w_q, w_o, sb_bias, norm_f):
    weights = (ada_w, ada_b, norm_g, ffn_w_gate, ffn_w_up, ffn_w_down, ssm_a_re, ssm_a_im,
               ssm_log_dt, ssm_b_re, ssm_b_im, ssm_c_re, ssm_c_im, ssm_d, glu_w, glu_b,
               kv_ada_w, kv_ada_b, kv_norm_g, w_kv, w_q, w_o, sb_bias, norm_f)
    h0 = jnp.zeros((N_A_LAYERS, x_prompt.shape[0], N_GROUPS, STATE_N), jnp.float32)
    y_prompt, k_prompt, v_prompt, ssm_re_prompt, ssm_im_prompt = _trunk(
        x_prompt, c_prompt, h0, h0, None, None, *weights)
    n_seq, n_pages = page_table.shape
    past_k = cache_k[page_table].reshape(n_seq, n_pages * PAGE_SIZE, N_HEADS, HEAD_DIM)
    past_v = cache_v[page_table].reshape(n_seq, n_pages * PAGE_SIZE, N_HEADS, HEAD_DIM)
    y_sample, k_sample, v_sample, ssm_re_sample, ssm_im_sample = _trunk(
        x_sample, c_sample, state_ssm_re, state_ssm_im, past_k, past_v, *weights)
    return (y_prompt, y_sample, k_prompt, v_prompt, k_sample, v_sample,
            ssm_re_prompt, ssm_im_prompt, ssm_re_sample, ssm_im_sample)
```

```python
import functools

import jax
import jax.numpy as jnp
from jax import lax
from jax.experimental import pallas as pl
from jax.experimental.pallas import tpu as pltpu

EPS = 1e-6
GROUP_CH = 16
N_MOD = 9
LANES = 128
V7X_VMEM_LIMIT = 56 * 1024 * 1024

F32 = jnp.float32
BF16 = jnp.bfloat16


def _dot(a, b):
    return jnp.dot(a, b, preferred_element_type=F32)


def _dot_nt(a, b):
    return lax.dot_general(a, b, (((1,), (1,)), ((), ())), preferred_element_type=F32)


def _split_bf16(x):
    hi = x.astype(BF16)
    lo = (x - hi.astype(F32)).astype(BF16)
    return hi, lo


def _mod_norm(x, g, shift, scale):
    xn = x * lax.rsqrt(jnp.mean(x * x, axis=-1, keepdims=True) + EPS)
    return (xn * g) * (1.0 + scale) + shift


def _params(sem):
    return pltpu.CompilerParams(dimension_semantics=sem, vmem_limit_bytes=V7X_VMEM_LIMIT)


def _resident(shape, index_map):
    return pl.BlockSpec(shape, index_map, pipeline_mode=pl.Buffered(1))


def _ada_kernel(c_ref, w_ref, b_ref, o_ref):
    c = c_ref[...]
    sc_hi, sc_lo = _split_bf16(c * jax.nn.sigmoid(c))
    w_hi, w_lo = _split_bf16(w_ref[...])
    o_ref[...] = _dot(sc_hi, w_hi) + (_dot(sc_hi, w_lo) + _dot(sc_lo, w_hi)) + b_ref[...]


def _ada_vectors(c, w, b):
    n_l, d, n = w.shape
    m = c.shape[0]
    tn = max(t for t in range(LANES, min(n, 1024) + 1, LANES) if n % t == 0)
    return pl.pallas_call(
        _ada_kernel,
        grid=(n_l, n // tn),
        in_specs=[pl.BlockSpec((m, d), lambda l, j: (0, 0)),
                  pl.BlockSpec((None, d, tn), lambda l, j: (l, 0, j)),
                  pl.BlockSpec((None, 1, tn), lambda l, j: (l, 0, j))],
        out_specs=pl.BlockSpec((None, m, tn), lambda l, j: (l, 0, j)),
        out_shape=jax.ShapeDtypeStruct((n_l, m, n), F32),
        compiler_params=_params(("arbitrary", "arbitrary")),
    )(c, w, b.reshape(n_l, 1, n))


def _row_specs(n_rows, seq_len, d, tm, per_row):
    x_spec = pl.BlockSpec((tm, d), lambda i: (i, 0))
    if per_row:
        return x_spec, x_spec
    blocks_per_seq = seq_len // tm
    return x_spec, pl.BlockSpec((None, 1, d), lambda i: (i // blocks_per_seq, 0, 0))


def _pick_tm(seq_len, n_rows, per_row, cap):
    if per_row:
        return n_rows
    tm = min(cap, seq_len)
    assert seq_len % tm == 0
    return tm


def _ffn_kernel(*refs, n_chunks, fc, post, emit_x):
    x_ref, sh_ref, sc_ref, gt_ref, g_ref, wg_ref, wu_ref, wd_ref = refs[:8]
    pos = 8
    if post == "mod":
        pg_ref, psh_ref, psc_ref = refs[pos:pos + 3]
        pos += 3
    elif post == "plain":
        pg_ref = refs[pos]
        pos += 1
    outs = list(refs[pos:-1])
    acc_ref = refs[-1]

    x = x_ref[...]
    h = _mod_norm(x, g_ref[...], sh_ref[...], sc_ref[...]).astype(BF16)
    for c in range(n_chunks):
        lo = c * fc
        gate = _dot(h, wg_ref[:, lo:lo + fc])
        up = _dot(h, wu_ref[:, lo:lo + fc])
        a = ((gate * jax.nn.sigmoid(gate)) * up).astype(BF16)
        part = _dot(a, wd_ref[lo:lo + fc, :])
        if c == 0:
            acc_ref[...] = part
        else:
            acc_ref[...] += part
    x_new = x + (0.5 * gt_ref[...]) * acc_ref[...]
    if emit_x:
        outs.pop(0)[...] = x_new
    if post == "mod":
        outs.pop(0)[...] = _mod_norm(x_new, pg_ref[...], psh_ref[...], psc_ref[...])
    elif post == "plain":
        xn = x_new * lax.rsqrt(jnp.mean(x_new * x_new, axis=-1, keepdims=True) + EPS)
        outs.pop(0)[...] = xn * pg_ref[...]


def _ffn(x, seq_len, shift, scale, gate, g, wg, wu, wd, *, per_row, post=None, post_args=(),
         post_time_major=False, emit_x=True):
    n_rows, d = x.shape
    f = wg.shape[1]
    tm = _pick_tm(seq_len, n_rows, per_row, 512)
    fc = 256 if f % 256 == 0 else f
    x_spec, v_spec = _row_specs(n_rows, seq_len, d, tm, per_row)
    g_spec = pl.BlockSpec((1, d), lambda i: (0, 0))
    in_specs = [x_spec, v_spec, v_spec, v_spec, g_spec,
                _resident((d, f), lambda i: (0, 0)), _resident((d, f), lambda i: (0, 0)),
                _resident((f, d), lambda i: (0, 0))]
    args = [x, shift, scale, gate, g.reshape(1, d), wg, wu, wd]
    if post == "mod":
        in_specs += [g_spec, v_spec, v_spec]
        args += [post_args[0].reshape(1, d), post_args[1], post_args[2]]
    elif post == "plain":
        in_specs += [g_spec]
        args += [post_args[0].reshape(1, d)]
    out_specs, out_shape = [], []
    if emit_x:
        out_specs.append(x_spec)
        out_shape.append(jax.ShapeDtypeStruct((n_rows, d), F32))
    if post is not None:
        if post_time_major:
            nt = seq_len // tm
            out_specs.append(pl.BlockSpec((tm, d), lambda i: (i % nt, i // nt)))
            out_shape.append(jax.ShapeDtypeStruct((seq_len, (n_rows // seq_len) * d), F32))
        else:
            out_specs.append(x_spec)
            out_shape.append(jax.ShapeDtypeStruct((n_rows, d), F32))
    outs = pl.pallas_call(
        functools.partial(_ffn_kernel, n_chunks=f // fc, fc=fc, post=post, emit_x=emit_x),
        grid=(n_rows // tm,),
        in_specs=in_specs,
        out_specs=out_specs,
        out_shape=out_shape,
        scratch_shapes=[pltpu.VMEM((tm, d), F32)],
        compiler_params=_params(("arbitrary",)),
    )(*args)
    return outs


def _ssm_kernel(u_ref, bmat_ref, cmat_ref, are_ref, aim_ref, d_ref, h0re_ref, h0im_ref,
                y_ref, sre_ref, sim_ref, hs_ref, hre_ref, him_ref, *, n_b, tt, n_half):
    j = pl.program_id(1)

    @pl.when(j == 0)
    def _():
        hre_ref[...] = h0re_ref[...]
        him_ref[...] = h0im_ref[...]

    u = u_ref[...]
    hs_ref[...] = _dot(u.astype(BF16), bmat_ref[...])
    a_re = jnp.broadcast_to(are_ref[...], (n_b, n_half))
    a_im = jnp.broadcast_to(aim_ref[...], (n_b, n_half))

    def step(t, carry):
        h_re, h_im = carry
        r0 = pl.multiple_of(t * n_b, n_b)
        bu_re = hs_ref[pl.ds(r0, n_b), 0:n_half]
        bu_im = hs_ref[pl.ds(r0, n_b), n_half:2 * n_half]
        new_re = (a_re * h_re - a_im * h_im) + bu_re
        new_im = (a_re * h_im + a_im * h_re) + bu_im
        hs_ref[pl.ds(r0, n_b), 0:n_half] = new_re
        hs_ref[pl.ds(r0, n_b), n_half:2 * n_half] = new_im
        return new_re, new_im

    h_re, h_im = lax.fori_loop(0, tt, step, (hre_ref[...], him_ref[...]))
    hre_ref[...] = h_re
    him_ref[...] = h_im
    y_ref[...] = _dot(hs_ref[...].astype(BF16), cmat_ref[...]) + d_ref[...] * u

    @pl.when(j == pl.num_programs(1) - 1)
    def _():
        sre_ref[...] = h_re
        sim_ref[...] = h_im


def _s5_tables(a_re, a_im, log_dt, b_re, b_im, c_re, c_im):
    n_g, n_s = a_re.shape
    gpb = LANES // GROUP_CH
    n_blk = n_g // gpb
    dt = jnp.exp(log_dt)[:, None]
    mag = jnp.exp(a_re * dt)
    abar_re = mag * jnp.cos(a_im * dt)
    abar_im = mag * jnp.sin(a_im * dt)
    den = a_re * a_re + a_im * a_im
    xr, xi = abar_re - 1.0, abar_im
    q_re = (xr * a_re + xi * a_im) / den
    q_im = (xi * a_re - xr * a_im) / den
    bb_re = q_re[:, :, None] * b_re - q_im[:, :, None] * b_im
    bb_im = q_re[:, :, None] * b_im + q_im[:, :, None] * b_re
    eye = jnp.eye(gpb, dtype=F32)

    def b_table(bb):
        t = bb.reshape(n_blk, gpb, n_s, GROUP_CH).transpose(0, 1, 3, 2)
        t = t[:, :, :, None, :] * eye[None, :, None, :, None]
        return t.reshape(n_blk, gpb * GROUP_CH, gpb * n_s)

    def c_table(cc):
        t = cc.reshape(n_blk, gpb, GROUP_CH, n_s).transpose(0, 1, 3, 2)
        t = t[:, :, :, None, :] * eye[None, :, None, :, None]
        return t.reshape(n_blk, gpb * n_s, gpb * GROUP_CH)

    bmat = jnp.concatenate([b_table(bb_re), b_table(bb_im)], axis=2).astype(BF16)
    cmat = jnp.concatenate([c_table(c_re), c_table(-c_im)], axis=1).astype(BF16)
    n_half = gpb * n_s
    return (bmat, cmat, abar_re.reshape(n_blk, 1, n_half), abar_im.reshape(n_blk, 1, n_half))


def _s5_scan(u_tm, n_b, tables, d_skip, h0_re, h0_im):
    bmat, cmat, abar_re, abar_im = tables
    n_rows, d = u_tm.shape
    t_len = n_rows // n_b
    n_blk, _, n_half = abar_re.shape
    tt = min(32, t_len)
    assert t_len % tt == 0
    st_spec = pl.BlockSpec((n_b, n_half), lambda k, j: (0, k))
    blk3 = lambda r, c: pl.BlockSpec((None, r, c), lambda k, j: (k, 0, 0))
    return pl.pallas_call(
        functools.partial(_ssm_kernel, n_b=n_b, tt=tt, n_half=n_half),
        grid=(n_blk, t_len // tt),
        in_specs=[pl.BlockSpec((tt * n_b, LANES), lambda k, j: (j, k)),
                  blk3(LANES, 2 * n_half), blk3(2 * n_half, LANES), blk3(1, n_half), blk3(1, n_half),
                  pl.BlockSpec((1, LANES), lambda k, j: (0, k)), st_spec, st_spec],
        out_specs=[pl.BlockSpec((tt * n_b, LANES), lambda k, j: (j, k)), st_spec, st_spec],
        out_shape=[jax.ShapeDtypeStruct((n_rows, d), F32),
                   jax.ShapeDtypeStruct((n_b, n_blk * n_half), F32),
                   jax.ShapeDtypeStruct((n_b, n_blk * n_half), F32)],
        scratch_shapes=[pltpu.VMEM((tt * n_b, 2 * n_half), F32),
                        pltpu.VMEM((n_b, n_half), F32), pltpu.VMEM((n_b, n_half), F32)],
        compiler_params=_params(("arbitrary", "arbitrary")),
    )(u_tm, bmat, cmat, abar_re, abar_im, d_skip.reshape(1, d), h0_re, h0_im)


def _glu_kernel(y_ref, x_ref, gt_ref, w_ref, b_ref, o_ref):
    d = x_ref.shape[-1]
    ge = jax.nn.gelu(y_ref[...]).astype(BF16)
    gl = _dot(ge, w_ref[...]) + b_ref[...]
    mix = gl[:, :d] * jax.nn.sigmoid(gl[:, d:])
    o_ref[...] = x_ref[...] + gt_ref[...] * mix


def _glu_residual(y, x, seq_len, gate, w, b, *, per_row, y_time_major):
    n_rows, d = x.shape
    tm = _pick_tm(seq_len, n_rows, per_row, 512)
    x_spec, v_spec = _row_specs(n_rows, seq_len, d, tm, per_row)
    if y_time_major:
        nt = seq_len // tm
        y_spec = pl.BlockSpec((tm, d), lambda i: (i % nt, i // nt))
    else:
        y_spec = x_spec
    return pl.pallas_call(
        _glu_kernel,
        grid=(n_rows // tm,),
        in_specs=[y_spec, x_spec, v_spec, _resident((d, 2 * d), lambda i: (0, 0)),
                  pl.BlockSpec((1, 2 * d), lambda i: (0, 0))],
        out_specs=x_spec,
        out_shape=jax.ShapeDtypeStruct((n_rows, d), F32),
        compiler_params=_params(("arbitrary",)),
    )(y, x, gate, w, b.reshape(1, 2 * d))


def _proj_kernel(x_ref, sh_ref, sc_ref, g_ref, w_ref, *o_refs, out_scale):
    d = x_ref.shape[-1]
    h = _mod_norm(x_ref[...], g_ref[...], sh_ref[...], sc_ref[...]).astype(BF16)
    for k, o_ref in enumerate(o_refs):
        o_ref[...] = (_dot(h, w_ref[:, k * d:(k + 1) * d]) * out_scale).astype(o_ref.dtype)


def _norm_proj(x, seq_len, shift, scale, g, w, *, per_row, out_dtype, out_scale=1.0):
    n_rows, d = x.shape
    n_out = w.shape[1] // d
    tm = _pick_tm(seq_len, n_rows, per_row, 512)
    x_spec, v_spec = _row_specs(n_rows, seq_len, d, tm, per_row)
    return pl.pallas_call(
        functools.partial(_proj_kernel, out_scale=out_scale),
        grid=(n_rows // tm,),
        in_specs=[x_spec, v_spec, v_spec, pl.BlockSpec((1, d), lambda i: (0, 0)),
                  _resident((d, n_out * d), lambda i: (0, 0))],
        out_specs=[x_spec] * n_out,
        out_shape=[jax.ShapeDtypeStruct((n_rows, d), out_dtype)] * n_out,
        compiler_params=_params(("arbitrary",)),
    )(x, shift, scale, g.reshape(1, d), w)


def _oproj_kernel(o_ref, x_ref, gt_ref, w_ref, out_ref):
    out_ref[...] = x_ref[...] + gt_ref[...] * _dot(o_ref[...], w_ref[...])


def _oproj_residual(o, x, seq_len, gate, w, *, per_row):
    n_rows, d = x.shape
    tm = _pick_tm(seq_len, n_rows, per_row, 512)
    x_spec, v_spec = _row_specs(n_rows, seq_len, d, tm, per_row)
    return pl.pallas_call(
        _oproj_kernel,
        grid=(n_rows // tm,),
        in_specs=[x_spec, x_spec, v_spec, _resident((d, d), lambda i: (0, 0))],
        out_specs=x_spec,
        out_shape=jax.ShapeDtypeStruct((n_rows, d), F32),
        compiler_params=_params(("arbitrary",)),
    )(o, x, gate, w)


def _log_sigmoid(z):
    return jnp.minimum(z, 0.0) - jnp.log1p(jnp.exp(-jnp.abs(z)))


def _suffix_sum_lanes(l1m, tri):
    hi, lo = _split_bf16(l1m)
    return _dot(hi, tri) + _dot(lo, tri)


def _prompt_attn_kernel(bias_ref, q_ref, k_ref, v_ref, o_ref, *, tq, hd):
    p = pl.program_id(1)
    i = pl.program_id(2)
    q2 = q_ref[...]
    lane = lax.broadcasted_iota(jnp.int32, (tq, LANES), 1)
    r_idx = lax.broadcasted_iota(jnp.int32, (tq, tq), 0)
    c_idx = lax.broadcasted_iota(jnp.int32, (tq, tq), 1)
    tri = (r_idx > c_idx).astype(BF16)
    valid = c_idx < r_idx

    accs = []
    for hh in range(LANES // hd):
        in_head = (lane >= hh * hd) & (lane < (hh + 1) * hd)
        qm = jnp.where(in_head, q2, jnp.zeros_like(q2))
        bias = bias_ref[p * (LANES // hd) + hh]

        def block(j, carry, masked):
            c_sum, acc = carry
            r0 = pl.multiple_of(j * tq, tq)
            ks = k_ref[pl.ds(r0, tq), :].astype(BF16)
            vs = v_ref[pl.ds(r0, tq), :].astype(BF16)
            z = _dot_nt(qm, ks) + bias
            log_b = _log_sigmoid(z)
            l1m = log_b - z
            if masked:
                l1m = jnp.where(valid, l1m, 0.0)
            suf = _suffix_sum_lanes(l1m, tri)
            w = jnp.exp(log_b + suf + c_sum)
            if masked:
                w = jnp.where(valid, w, 0.0)
            acc = acc + _dot(w.astype(BF16), vs)
            c_sum = c_sum + (suf[:, 0:1] + l1m[:, 0:1])
            return c_sum, acc

        carry = (jnp.zeros((tq, 1), F32), jnp.zeros((tq, LANES), F32))
        carry = block(i, carry, True)
        carry = lax.fori_loop(0, i, lambda jj, cr: block(i - 1 - jj, cr, False), carry)
        accs.append((in_head, carry[1]))

    out = jnp.zeros((tq, LANES), F32)
    for in_head, acc in accs:
        out = jnp.where(in_head, acc, out)
    o_ref[...] = out.astype(o_ref.dtype)


def _prompt_attention(q, k, v, bias, n_seq, seq_len, hd):
    n_rows, d = q.shape
    tq = min(256, seq_len)
    nq = seq_len // tq
    n_pair = d // LANES
    return pl.pallas_call(
        functools.partial(_prompt_attn_kernel, tq=tq, hd=hd),
        grid=(n_seq, n_pair, nq),
        in_specs=[pl.BlockSpec(memory_space=pltpu.SMEM),
                  pl.BlockSpec((tq, LANES), lambda b, p, i: (b * nq + i, p)),
                  pl.BlockSpec((seq_len, LANES), lambda b, p, i: (b, p)),
                  pl.BlockSpec((seq_len, LANES), lambda b, p, i: (b, p))],
        out_specs=pl.BlockSpec((tq, LANES), lambda b, p, i: (b * nq + i, p)),
        out_shape=jax.ShapeDtypeStruct((n_rows, d), BF16),
        compiler_params=_params(("arbitrary", "arbitrary", "arbitrary")),
    )(bias, q, k, v)


def _sample_attn_kernel(pt_ref, qbd_ref, bias_ref, kn_ref, vn_ref, *rest, n_pp, page, n_q, hd):
    k_refs = rest[:n_pp]
    v_refs = rest[n_pp:2 * n_pp]
    o_ref, acc_ref, c_ref = rest[2 * n_pp:]
    j = pl.program_id(1)
    n_col = qbd_ref.shape[-1]
    d = qbd_ref.shape[0]
    qbd = qbd_ref[...]
    bias = bias_ref[...]
    r_idx = lax.broadcasted_iota(jnp.int32, (page, page), 0)
    c_idx = lax.broadcasted_iota(jnp.int32, (page, page), 1)
    tri = (c_idx > r_idx).astype(BF16)

    def one_page(k_page, v_page, valid):
        z = _dot(k_page.astype(BF16), qbd) + bias
        log_b = _log_sigmoid(z)
        l1m = log_b - z
        if valid is not None:
            l1m = jnp.where(valid, l1m, 0.0)
        hi, lo = _split_bf16(l1m)
        suf = _dot(tri, hi) + _dot(tri, lo)
        w = jnp.exp(log_b + suf + c_ref[...])
        if valid is not None:
            w = jnp.where(valid, w, 0.0)
        acc_ref[...] += _dot(w.T.astype(BF16), v_page.astype(BF16))
        c_ref[...] += suf[0:1, :] + l1m[0:1, :]

    @pl.when(j == 0)
    def _():
        acc_ref[...] = jnp.zeros_like(acc_ref)
        c_ref[...] = jnp.zeros_like(c_ref)
        key_i = lax.broadcasted_iota(jnp.int32, (page, n_col), 0)
        qry_i = lax.broadcasted_iota(jnp.int32, (page, n_col), 1) % n_q
        one_page(kn_ref[...], vn_ref[...], key_i < qry_i)

    @pl.when(j > 0)
    def _():
        for pp in range(n_pp):
            one_page(k_refs[pp][...], v_refs[pp][...], None)

    @pl.when(j == pl.num_programs(1) - 1)
    def _():
        n_h = n_col // n_q
        row_h = lax.broadcasted_iota(jnp.int32, (n_col, d), 0) // n_q
        col_h = lax.broadcasted_iota(jnp.int32, (n_col, d), 1) // hd
        picked = jnp.where(row_h == col_h, acc_ref[...], 0.0)
        out = picked[0:n_q, :]
        for h in range(1, n_h):
            out = out + picked[h * n_q:(h + 1) * n_q, :]
        o_ref[...] = out.astype(o_ref.dtype)


def _sample_attention(q, k_new, v_new, cache_k, cache_v, page_table, bias, n_q, hd):
    n_rows, d = q.shape
    n_seq = n_rows // n_q
    n_h = d // hd
    n_col = n_h * n_q
    n_pool, page, _ = cache_k.shape
    n_pages = page_table.shape[1]
    n_pp = 4 if n_pages % 4 == 0 else 1
    n_steps = n_pages // n_pp
    q4 = q.reshape(n_seq, n_q, n_h, hd).transpose(0, 2, 3, 1)
    qbd = (q4[:, :, :, None, :] * jnp.eye(n_h, dtype=q.dtype)[None, :, None, :, None]).reshape(n_seq, d, n_col)
    bias_cols = jnp.repeat(bias.astype(F32), n_q).reshape(1, n_col)
    pad = lambda a: jnp.pad(a.reshape(n_seq, n_q, d), ((0, 0), (0, page - n_q), (0, 0)))
    seq_spec = lambda r, c: pl.BlockSpec((None, r, c), lambda b, j, pt: (b, 0, 0))

    def page_spec(pp):
        def index_map(b, j, pt):
            group = jnp.maximum(j, 1) - 1
            return (pt[b, n_pages - 1 - group * n_pp - pp], 0, 0)
        return pl.BlockSpec((None, page, d), index_map)

    grid_spec = pltpu.PrefetchScalarGridSpec(
        num_scalar_prefetch=1,
        grid=(n_seq, n_steps + 1),
        in_specs=[seq_spec(d, n_col), pl.BlockSpec((1, n_col), lambda b, j, pt: (0, 0)),
                  seq_spec(page, d), seq_spec(page, d)]
                 + [page_spec(pp) for pp in range(n_pp)] * 2,
        out_specs=seq_spec(n_q, d),
        scratch_shapes=[pltpu.VMEM((n_col, d), F32), pltpu.VMEM((1, n_col), F32)],
    )
    out = pl.pallas_call(
        functools.partial(_sample_attn_kernel, n_pp=n_pp, page=page, n_q=n_q, hd=hd),
        grid_spec=grid_spec,
        out_shape=jax.ShapeDtypeStruct((n_seq, n_q, d), BF16),
        compiler_params=_params(("arbitrary", "arbitrary")),
    )(page_table, qbd, bias_cols, pad(k_new), pad(v_new), *([cache_k] * n_pp), *([cache_v] * n_pp))
    return out.reshape(n_rows, d)


def _trunk(x3, mods, kv_mod, per_row, h0_re, h0_im, past, w):
    n_seq, t_len, d = x3.shape
    n_layers = len(mods)
    n_a = n_layers // 2
    hd = d // w["sb_bias"].shape[1]
    x = x3.reshape(n_seq * t_len, d)
    new_re, new_im = [], []
    k_new = v_new = None
    for l in range(n_layers):
        m = mods[l]
        g = w["norm_g"][l]
        if l == n_a:
            k_new, v_new = _norm_proj(x, t_len, kv_mod[0], kv_mod[1], w["kv_norm_g"], w["w_kv"],
                                      per_row=per_row, out_dtype=F32)
        ffn1 = (w["ffn_w_gate"][l][0], w["ffn_w_up"][l][0], w["ffn_w_down"][l][0])
        ffn2 = (w["ffn_w_gate"][l][1], w["ffn_w_up"][l][1], w["ffn_w_down"][l][1])
        if l < n_a:
            x, u = _ffn(x, t_len, m[0], m[1], m[2], g[0], *ffn1, per_row=per_row, post="mod",
                        post_args=(g[1], m[3], m[4]), post_time_major=not per_row)
            if per_row:
                u = u.reshape(n_seq, t_len, d).transpose(1, 0, 2)
            u_tm = u.reshape(t_len * n_seq, d)
            y_tm, s_re, s_im = _s5_scan(u_tm, n_seq, w["s5_tables"][l], w["ssm_d"][l], h0_re[l], h0_im[l])
            if per_row:
                y = y_tm.reshape(t_len, n_seq, d).transpose(1, 0, 2).reshape(n_seq * t_len, d)
            else:
                y = y_tm.reshape(t_len, n_seq * d)
            x = _glu_residual(y, x, t_len, m[5], w["glu_w"][l], w["glu_b"][l], per_row=per_row,
                              y_time_major=not per_row)
            new_re.append(s_re)
            new_im.append(s_im)
        else:
            jb = l - n_a
            (x,) = _ffn(x, t_len, m[0], m[1], m[2], g[0], *ffn1, per_row=per_row)
            (q,) = _norm_proj(x, t_len, m[3], m[4], g[1], w["w_q"][jb], per_row=per_row, out_dtype=BF16,
                              out_scale=hd ** -0.5)
            if past is None:
                o = _prompt_attention(q, k_new, v_new, w["sb_bias"][jb], n_seq, t_len, hd)
            else:
                o = _sample_attention(q, k_new, v_new, past[0], past[1], past[2], w["sb_bias"][jb], t_len, hd)
            x = _oproj_residual(o, x, t_len, m[5], w["w_o"][jb], per_row=per_row)
        if l == n_layers - 1:
            (x,) = _ffn(x, t_len, m[6], m[7], m[8], g[2], *ffn2, per_row=per_row, post="plain",
                        post_args=(w["norm_f"],), emit_x=False)
        else:
            (x,) = _ffn(x, t_len, m[6], m[7], m[8], g[2], *ffn2, per_row=per_row)
    return x, k_new, v_new, new_re, new_im


def kernel(x_prompt, x_sample, cache_k, cache_v, state_ssm_re, state_ssm_im, page_table,
           c_prompt, c_sample, ada_w, ada_b, norm_g, ffn_w_gate, ffn_w_up, ffn_w_down,
           ssm_a_re, ssm_a_im, ssm_log_dt, ssm_b_re, ssm_b_im, ssm_c_re, ssm_c_im, ssm_d,
           glu_w, glu_b, kv_ada_w, kv_ada_b, kv_norm_g, w_kv, w_q, w_o, sb_bias, norm_f):
    n_p, t_p, d = x_prompt.shape
    n_s, t_s, _ = x_sample.shape
    n_layers = ada_w.shape[0]
    n_a = n_layers // 2
    n_heads = sb_bias.shape[1]
    hd = d // n_heads
    n_pool, page = cache_k.shape[:2]

    c_all = jnp.concatenate([c_prompt, c_sample], axis=0)
    mod_all = _ada_vectors(c_all, ada_w, ada_b)
    kv_all = _ada_vectors(c_all, kv_ada_w[None], kv_ada_b[None])[0]

    def split(vecs, n_vec):
        pr = [vecs[:n_p, k * d:(k + 1) * d].reshape(n_p, 1, d) for k in range(n_vec)]
        sa = [jnp.repeat(vecs[n_p:, k * d:(k + 1) * d], t_s, axis=0) for k in range(n_vec)]
        return pr, sa

    mods_p, mods_s = zip(*[split(mod_all[l], N_MOD) for l in range(n_layers)])
    kv_p, kv_s = split(kv_all, 2)

    w = dict(
        norm_g=norm_g, kv_norm_g=kv_norm_g, norm_f=norm_f, ssm_d=ssm_d, glu_b=glu_b, sb_bias=sb_bias,
        ffn_w_gate=ffn_w_gate.astype(BF16), ffn_w_up=ffn_w_up.astype(BF16), ffn_w_down=ffn_w_down.astype(BF16),
        glu_w=glu_w.astype(BF16), w_kv=w_kv.astype(BF16), w_q=w_q.astype(BF16), w_o=w_o.astype(BF16),
        s5_tables=[_s5_tables(ssm_a_re[l], ssm_a_im[l], ssm_log_dt[l], ssm_b_re[l], ssm_b_im[l],
                              ssm_c_re[l], ssm_c_im[l]) for l in range(n_a)],
    )

    zeros = jnp.zeros((n_a, n_p, state_ssm_re.shape[2] * state_ssm_re.shape[3]), F32)
    y_p, k_p, v_p, re_p, im_p = _trunk(x_prompt, mods_p, kv_p, False, zeros, zeros, None, w)

    h0_re = state_ssm_re.reshape(n_a, n_s, -1)
    h0_im = state_ssm_im.reshape(n_a, n_s, -1)
    past = (cache_k.reshape(n_pool, page, d), cache_v.reshape(n_pool, page, d), page_table)
    y_s, k_s, v_s, re_s, im_s = _trunk(x_sample, mods_s, kv_s, True, h0_re, h0_im, past, w)

    st_shape_p = (n_a, n_p) + state_ssm_re.shape[2:]
    st_shape_s = (n_a, n_s) + state_ssm_re.shape[2:]
    return (y_p.reshape(n_p, t_p, d), y_s.reshape(n_s, t_s, d),
            k_p.reshape(n_p, t_p, n_heads, hd), v_p.reshape(n_p, t_p, n_heads, hd),
            k_s.reshape(n_s, t_s, n_heads, hd), v_s.reshape(n_s, t_s, n_heads, hd),
            jnp.stack(re_p).reshape(st_shape_p), jnp.stack(im_p).reshape(st_shape_p),
            jnp.stack(re_s).reshape(st_shape_s), jnp.stack(im_s).reshape(st_shape_s))
```

```python
import functools

import jax
import jax.numpy as jnp
from jax import lax
from jax.experimental import pallas as pl
from jax.experimental.pallas import tpu as pltpu

EPS = 1e-6
GROUP_CH = 16
N_MOD = 9
LANES = 128
MXU_TILE = 256
V7X_VMEM_LIMIT = 56 * 1024 * 1024

F32 = jnp.float32
BF16 = jnp.bfloat16


def _dot(a, b):
    return jnp.dot(a, b, preferred_element_type=F32)


def _dot_nt(a, b):
    return lax.dot_general(a, b, (((1,), (1,)), ((), ())), preferred_element_type=F32)


def _split_bf16(x):
    hi = x.astype(BF16)
    lo = (x - hi.astype(F32)).astype(BF16)
    return hi, lo


def _mod_norm(x, g, shift, scale):
    xn = x * lax.rsqrt(jnp.mean(x * x, axis=-1, keepdims=True) + EPS)
    return (xn * g) * (1.0 + scale) + shift


def _params(sem):
    return pltpu.CompilerParams(dimension_semantics=sem, vmem_limit_bytes=V7X_VMEM_LIMIT)


def _resident(shape, index_map):
    return pl.BlockSpec(shape, index_map, pipeline_mode=pl.Buffered(1))


def _ada_kernel(c_ref, w_ref, b_ref, o_ref):
    c = c_ref[...]
    sc_hi, sc_lo = _split_bf16(c * jax.nn.sigmoid(c))
    w_hi, w_lo = _split_bf16(w_ref[...])
    o_ref[...] = _dot(sc_hi, w_hi) + (_dot(sc_hi, w_lo) + _dot(sc_lo, w_hi)) + b_ref[...]


def _ada_vectors(c, w, b):
    n_l, d, n = w.shape
    m = c.shape[0]
    tn = max(t for t in range(LANES, min(n, 1024) + 1, LANES) if n % t == 0)
    return pl.pallas_call(
        _ada_kernel,
        grid=(n_l, n // tn),
        in_specs=[pl.BlockSpec((m, d), lambda l, j: (0, 0)),
                  pl.BlockSpec((None, d, tn), lambda l, j: (l, 0, j)),
                  pl.BlockSpec((None, 1, tn), lambda l, j: (l, 0, j))],
        out_specs=pl.BlockSpec((None, m, tn), lambda l, j: (l, 0, j)),
        out_shape=jax.ShapeDtypeStruct((n_l, m, n), F32),
        compiler_params=_params(("arbitrary", "arbitrary")),
    )(c, w, b.reshape(n_l, 1, n))


def _row_specs(n_rows, seq_len, d, tm, per_row):
    x_spec = pl.BlockSpec((tm, d), lambda i: (i, 0))
    if per_row:
        return x_spec, x_spec
    blocks_per_seq = seq_len // tm
    return x_spec, pl.BlockSpec((None, 1, d), lambda i: (i // blocks_per_seq, 0, 0))


def _pick_tm(seq_len, n_rows, per_row, cap):
    if per_row:
        return n_rows
    tm = min(cap, seq_len)
    assert seq_len % tm == 0
    return tm


def _ffn_kernel(*refs, n_chunks, fc, post, emit_x):
    x_ref, sh_ref, sc_ref, gt_ref, g_ref, wg_ref, wu_ref, wd_ref = refs[:8]
    pos = 8
    if post == "mod":
        pg_ref, psh_ref, psc_ref = refs[pos:pos + 3]
        pos += 3
    elif post == "plain":
        pg_ref = refs[pos]
        pos += 1
    outs = list(refs[pos:-1])
    acc_ref = refs[-1]

    x = x_ref[...]
    h = _mod_norm(x, g_ref[...], sh_ref[...], sc_ref[...]).astype(BF16)
    for c in range(n_chunks):
        lo = c * fc
        gate = _dot(h, wg_ref[:, lo:lo + fc])
        up = _dot(h, wu_ref[:, lo:lo + fc])
        a = ((gate * jax.nn.sigmoid(gate)) * up).astype(BF16)
        part = _dot(a, wd_ref[lo:lo + fc, :])
        if c == 0:
            acc_ref[...] = part
        else:
            acc_ref[...] += part
    x_new = x + (0.5 * gt_ref[...]) * acc_ref[...]
    if emit_x:
        outs.pop(0)[...] = x_new
    if post == "mod":
        outs.pop(0)[...] = _mod_norm(x_new, pg_ref[...], psh_ref[...], psc_ref[...])
    elif post == "plain":
        xn = x_new * lax.rsqrt(jnp.mean(x_new * x_new, axis=-1, keepdims=True) + EPS)
        outs.pop(0)[...] = xn * pg_ref[...]


def _ffn(x, seq_len, shift, scale, gate, g, wg, wu, wd, *, per_row, post=None, post_args=(),
         post_time_major=False, emit_x=True):
    n_rows, d = x.shape
    f = wg.shape[1]
    tm = _pick_tm(seq_len, n_rows, per_row, 512)
    fc = 256 if f % 256 == 0 else f
    x_spec, v_spec = _row_specs(n_rows, seq_len, d, tm, per_row)
    g_spec = pl.BlockSpec((1, d), lambda i: (0, 0))
    in_specs = [x_spec, v_spec, v_spec, v_spec, g_spec,
                _resident((d, f), lambda i: (0, 0)), _resident((d, f), lambda i: (0, 0)),
                _resident((f, d), lambda i: (0, 0))]
    args = [x, shift, scale, gate, g.reshape(1, d), wg, wu, wd]
    if post == "mod":
        in_specs += [g_spec, v_spec, v_spec]
        args += [post_args[0].reshape(1, d), post_args[1], post_args[2]]
    elif post == "plain":
        in_specs += [g_spec]
        args += [post_args[0].reshape(1, d)]
    out_specs, out_shape = [], []
    if emit_x:
        out_specs.append(x_spec)
        out_shape.append(jax.ShapeDtypeStruct((n_rows, d), F32))
    if post is not None:
        if post_time_major:
            nt = seq_len // tm
            out_specs.append(pl.BlockSpec((tm, d), lambda i: (i % nt, i // nt)))
            out_shape.append(jax.ShapeDtypeStruct((seq_len, (n_rows // seq_len) * d), F32))
        else:
            out_specs.append(x_spec)
            out_shape.append(jax.ShapeDtypeStruct((n_rows, d), F32))
    outs = pl.pallas_call(
        functools.partial(_ffn_kernel, n_chunks=f // fc, fc=fc, post=post, emit_x=emit_x),
        grid=(n_rows // tm,),
        in_specs=in_specs,
        out_specs=out_specs,
        out_shape=out_shape,
        scratch_shapes=[pltpu.VMEM((tm, d), F32)],
        compiler_params=_params(("arbitrary",)),
    )(*args)
    return outs


def _ssm_kernel(u_ref, bmat_ref, cmat_ref, are_ref, aim_ref, d_ref, h0re_ref, h0im_ref,
                y_ref, sre_ref, sim_ref, hs_ref, hre_ref, him_ref, *, n_b, tt, n_half):
    j = pl.program_id(1)

    @pl.when(j == 0)
    def _():
        hre_ref[...] = h0re_ref[...]
        him_ref[...] = h0im_ref[...]

    u = u_ref[...]
    hs_ref[...] = _dot(u.astype(BF16), bmat_ref[...])
    a_re = jnp.broadcast_to(are_ref[...], (n_b, n_half))
    a_im = jnp.broadcast_to(aim_ref[...], (n_b, n_half))

    def step(t, carry):
        h_re, h_im = carry
        r0 = pl.multiple_of(t * n_b, n_b)
        bu_re = hs_ref[pl.ds(r0, n_b), 0:n_half]
        bu_im = hs_ref[pl.ds(r0, n_b), n_half:2 * n_half]
        new_re = (a_re * h_re - a_im * h_im) + bu_re
        new_im = (a_re * h_im + a_im * h_re) + bu_im
        hs_ref[pl.ds(r0, n_b), 0:n_half] = new_re
        hs_ref[pl.ds(r0, n_b), n_half:2 * n_half] = new_im
        return new_re, new_im

    h_re, h_im = lax.fori_loop(0, tt, step, (hre_ref[...], him_ref[...]))
    hre_ref[...] = h_re
    him_ref[...] = h_im
    y_ref[...] = _dot(hs_ref[...].astype(BF16), cmat_ref[...]) + d_ref[...] * u

    @pl.when(j == pl.num_programs(1) - 1)
    def _():
        sre_ref[...] = h_re
        sim_ref[...] = h_im


def _s5_tables(a_re, a_im, log_dt, b_re, b_im, c_re, c_im):
    n_g, n_s = a_re.shape
    gpb = LANES // GROUP_CH
    n_blk = n_g // gpb
    dt = jnp.exp(log_dt)[:, None]
    mag = jnp.exp(a_re * dt)
    abar_re = mag * jnp.cos(a_im * dt)
    abar_im = mag * jnp.sin(a_im * dt)
    den = a_re * a_re + a_im * a_im
    xr, xi = abar_re - 1.0, abar_im
    q_re = (xr * a_re + xi * a_im) / den
    q_im = (xi * a_re - xr * a_im) / den
    bb_re = q_re[:, :, None] * b_re - q_im[:, :, None] * b_im
    bb_im = q_re[:, :, None] * b_im + q_im[:, :, None] * b_re
    eye = jnp.eye(gpb, dtype=F32)

    def b_table(bb):
        t = bb.reshape(n_blk, gpb, n_s, GROUP_CH).transpose(0, 1, 3, 2)
        t = t[:, :, :, None, :] * eye[None, :, None, :, None]
        return t.reshape(n_blk, gpb * GROUP_CH, gpb * n_s)

    def c_table(cc):
        t = cc.reshape(n_blk, gpb, GROUP_CH, n_s).transpose(0, 1, 3, 2)
        t = t[:, :, :, None, :] * eye[None, :, None, :, None]
        return t.reshape(n_blk, gpb * n_s, gpb * GROUP_CH)

    bmat = jnp.concatenate([b_table(bb_re), b_table(bb_im)], axis=2).astype(BF16)
    cmat = jnp.concatenate([c_table(c_re), c_table(-c_im)], axis=1).astype(BF16)
    n_half = gpb * n_s
    return (bmat, cmat, abar_re.reshape(n_blk, 1, n_half), abar_im.reshape(n_blk, 1, n_half))


def _s5_scan(u_tm, n_b, tables, d_skip, h0_re, h0_im):
    bmat, cmat, abar_re, abar_im = tables
    n_rows, d = u_tm.shape
    t_len = n_rows // n_b
    n_blk, _, n_half = abar_re.shape
    tt = min(32, t_len)
    assert t_len % tt == 0
    st_spec = pl.BlockSpec((n_b, n_half), lambda k, j: (0, k))
    blk3 = lambda r, c: pl.BlockSpec((None, r, c), lambda k, j: (k, 0, 0))
    return pl.pallas_call(
        functools.partial(_ssm_kernel, n_b=n_b, tt=tt, n_half=n_half),
        grid=(n_blk, t_len // tt),
        in_specs=[pl.BlockSpec((tt * n_b, LANES), lambda k, j: (j, k)),
                  blk3(LANES, 2 * n_half), blk3(2 * n_half, LANES), blk3(1, n_half), blk3(1, n_half),
                  pl.BlockSpec((1, LANES), lambda k, j: (0, k)), st_spec, st_spec],
        out_specs=[pl.BlockSpec((tt * n_b, LANES), lambda k, j: (j, k)), st_spec, st_spec],
        out_shape=[jax.ShapeDtypeStruct((n_rows, d), F32),
                   jax.ShapeDtypeStruct((n_b, n_blk * n_half), F32),
                   jax.ShapeDtypeStruct((n_b, n_blk * n_half), F32)],
        scratch_shapes=[pltpu.VMEM((tt * n_b, 2 * n_half), F32),
                        pltpu.VMEM((n_b, n_half), F32), pltpu.VMEM((n_b, n_half), F32)],
        compiler_params=_params(("arbitrary", "arbitrary")),
    )(u_tm, bmat, cmat, abar_re, abar_im, d_skip.reshape(1, d), h0_re, h0_im)


def _glu_kernel(y_ref, x_ref, gt_ref, w_ref, b_ref, o_ref):
    d = x_ref.shape[-1]
    ge = jax.nn.gelu(y_ref[...]).astype(BF16)
    gl = _dot(ge, w_ref[...]) + b_ref[...]
    mix = gl[:, :d] * jax.nn.sigmoid(gl[:, d:])
    o_ref[...] = x_ref[...] + gt_ref[...] * mix


def _glu_residual(y, x, seq_len, gate, w, b, *, per_row, y_time_major):
    n_rows, d = x.shape
    tm = _pick_tm(seq_len, n_rows, per_row, 512)
    x_spec, v_spec = _row_specs(n_rows, seq_len, d, tm, per_row)
    if y_time_major:
        nt = seq_len // tm
        y_spec = pl.BlockSpec((tm, d), lambda i: (i % nt, i // nt))
    else:
        y_spec = x_spec
    return pl.pallas_call(
        _glu_kernel,
        grid=(n_rows // tm,),
        in_specs=[y_spec, x_spec, v_spec, _resident((d, 2 * d), lambda i: (0, 0)),
                  pl.BlockSpec((1, 2 * d), lambda i: (0, 0))],
        out_specs=x_spec,
        out_shape=jax.ShapeDtypeStruct((n_rows, d), F32),
        compiler_params=_params(("arbitrary",)),
    )(y, x, gate, w, b.reshape(1, 2 * d))


def _proj_kernel(x_ref, sh_ref, sc_ref, g_ref, w_ref, *o_refs, out_scale):
    d = x_ref.shape[-1]
    h = _mod_norm(x_ref[...], g_ref[...], sh_ref[...], sc_ref[...]).astype(BF16)
    for k, o_ref in enumerate(o_refs):
        o_ref[...] = (_dot(h, w_ref[:, k * d:(k + 1) * d]) * out_scale).astype(o_ref.dtype)


def _norm_proj(x, seq_len, shift, scale, g, w, *, per_row, out_dtype, out_scale=1.0):
    n_rows, d = x.shape
    n_out = w.shape[1] // d
    tm = _pick_tm(seq_len, n_rows, per_row, 512)
    x_spec, v_spec = _row_specs(n_rows, seq_len, d, tm, per_row)
    return pl.pallas_call(
        functools.partial(_proj_kernel, out_scale=out_scale),
        grid=(n_rows // tm,),
        in_specs=[x_spec, v_spec, v_spec, pl.BlockSpec((1, d), lambda i: (0, 0)),
                  _resident((d, n_out * d), lambda i: (0, 0))],
        out_specs=[x_spec] * n_out,
        out_shape=[jax.ShapeDtypeStruct((n_rows, d), out_dtype)] * n_out,
        compiler_params=_params(("arbitrary",)),
    )(x, shift, scale, g.reshape(1, d), w)


def _proj_t_kernel(x_ref, sh_ref, sc_ref, g_ref, wt_ref, *o_refs):
    d = x_ref.shape[-1]
    h = _mod_norm(x_ref[...], g_ref[...], sh_ref[...], sc_ref[...]).astype(BF16)
    for k, o_ref in enumerate(o_refs):
        o_ref[...] = _dot_nt(wt_ref[k * d:(k + 1) * d, :], h)


def _norm_proj_t(x, n_seq, seq_len, shift, scale, g, wt):
    n_rows, d = x.shape
    n_out = wt.shape[0] // d
    tm = _pick_tm(seq_len, n_rows, False, 512)
    nt = seq_len // tm
    x_spec, v_spec = _row_specs(n_rows, seq_len, d, tm, False)
    o_spec = pl.BlockSpec((None, d, tm), lambda i: (i // nt, 0, i % nt))
    return pl.pallas_call(
        _proj_t_kernel,
        grid=(n_rows // tm,),
        in_specs=[x_spec, v_spec, v_spec, pl.BlockSpec((1, d), lambda i: (0, 0)),
                  _resident((n_out * d, d), lambda i: (0, 0))],
        out_specs=[o_spec] * n_out,
        out_shape=[jax.ShapeDtypeStruct((n_seq, d, seq_len), F32)] * n_out,
        compiler_params=_params(("arbitrary",)),
    )(x, shift, scale, g.reshape(1, d), wt)


def _oproj_kernel(o_ref, x_ref, gt_ref, w_ref, out_ref):
    out_ref[...] = x_ref[...] + gt_ref[...] * _dot(o_ref[...], w_ref[...])


def _oproj_residual(o, x, seq_len, gate, w, *, per_row):
    n_rows, d = x.shape
    tm = _pick_tm(seq_len, n_rows, per_row, 512)
    x_spec, v_spec = _row_specs(n_rows, seq_len, d, tm, per_row)
    return pl.pallas_call(
        _oproj_kernel,
        grid=(n_rows // tm,),
        in_specs=[x_spec, x_spec, v_spec, _resident((d, d), lambda i: (0, 0))],
        out_specs=x_spec,
        out_shape=jax.ShapeDtypeStruct((n_rows, d), F32),
        compiler_params=_params(("arbitrary",)),
    )(o, x, gate, w)


def _log_beta_pair(z):
    sp = jnp.log(1.0 + jnp.exp(-jnp.abs(z)))
    return jnp.minimum(z, 0.0) - sp, -jnp.maximum(z, 0.0) - sp


def _suffix_sum_lanes(l1m, tri):
    hi, lo = _split_bf16(l1m)
    return _dot(hi, tri) + _dot(lo, tri)


def _prompt_attn_kernel(bias_ref, q_ref, kt_ref, vt_ref, o_ref, ktb_ref, vtm_ref, *, tq, hd):
    grp = pl.program_id(1)
    i = pl.program_id(2)
    n_hh = MXU_TILE // hd

    @pl.when(i == 0)
    def _():
        ktb_ref[...] = kt_ref[...].astype(BF16)
        vt = vt_ref[...]
        ch_head = lax.broadcasted_iota(jnp.int32, vt.shape, 0) // hd
        for hh in range(n_hh):
            vtm_ref[hh] = jnp.where(ch_head == hh, vt, 0.0).astype(BF16)

    q2 = q_ref[...]
    lane_head = lax.broadcasted_iota(jnp.int32, (tq, MXU_TILE), 1) // hd
    r_idx = lax.broadcasted_iota(jnp.int32, (tq, tq), 0)
    c_idx = lax.broadcasted_iota(jnp.int32, (tq, tq), 1)
    tri = (r_idx > c_idx).astype(BF16)
    valid = c_idx < r_idx
    qns = [jnp.where(lane_head == hh, -q2, jnp.zeros_like(q2)) for hh in range(n_hh)]
    biases = [bias_ref[grp * n_hh + hh] for hh in range(n_hh)]
    heads = range(n_hh)

    def block(j, carry, masked):
        c_sums, acc = carry
        c0 = pl.multiple_of(j * tq, tq)
        kt = ktb_ref[:, pl.ds(c0, tq)]
        zns = [_dot(qns[hh], kt) - biases[hh] for hh in heads]
        l1ms = []
        for hh in heads:
            l1m = jnp.minimum(zns[hh], 0.0) - jnp.log(1.0 + jnp.exp(-jnp.abs(zns[hh])))
            l1ms.append(jnp.where(valid, l1m, 0.0) if masked else l1m)
        sufs = [_suffix_sum_lanes(l1ms[hh], tri) for hh in heads]
        ws = []
        for hh in heads:
            w = jnp.exp((l1ms[hh] - zns[hh]) + sufs[hh] + c_sums[hh])
            ws.append((jnp.where(valid, w, 0.0) if masked else w).astype(BF16))
        for hh in heads:
            acc = acc + _dot_nt(ws[hh], vtm_ref[hh, :, pl.ds(c0, tq)])
        new_sums = tuple(c_sums[hh] + (sufs[hh][:, 0:1] + l1ms[hh][:, 0:1]) for hh in heads)
        return new_sums, acc

    carry = (tuple(jnp.zeros((tq, 1), F32) for _ in range(n_hh)), jnp.zeros((tq, MXU_TILE), F32))
    carry = block(i, carry, True)
    carry = lax.fori_loop(0, i, lambda jj, cr: block(i - 1 - jj, cr, False), carry)
    o_ref[...] = carry[1].astype(o_ref.dtype)


def _prompt_attention(q, kt, vt, bias, n_seq, seq_len, hd):
    n_rows, d = q.shape
    tq = min(256, seq_len)
    nq = seq_len // tq
    n_grp = d // MXU_TILE
    kv_spec = pl.BlockSpec((None, MXU_TILE, seq_len), lambda b, p, i: (b, p, 0))
    return pl.pallas_call(
        functools.partial(_prompt_attn_kernel, tq=tq, hd=hd),
        grid=(n_seq, n_grp, nq),
        in_specs=[pl.BlockSpec(memory_space=pltpu.SMEM),
                  pl.BlockSpec((tq, MXU_TILE), lambda b, p, i: (b * nq + i, p)),
                  kv_spec, kv_spec],
        out_specs=pl.BlockSpec((tq, MXU_TILE), lambda b, p, i: (b * nq + i, p)),
        out_shape=jax.ShapeDtypeStruct((n_rows, d), BF16),
        scratch_shapes=[pltpu.VMEM((MXU_TILE, seq_len), BF16),
                        pltpu.VMEM((MXU_TILE // hd, MXU_TILE, seq_len), BF16)],
        compiler_params=_params(("arbitrary", "arbitrary", "arbitrary")),
    )(bias, q, kt, vt)


def _sample_attn_kernel(pt_ref, qb_ref, bias_ref, knt_ref, vnt_ref, *rest, n_pp, page, n_q, hd):
    k_refs = rest[:n_pp]
    v_refs = rest[n_pp:2 * n_pp]
    o_ref, acc_ref, c_ref = rest[2 * n_pp:]
    j = pl.program_id(1)
    n_tile, rows_t, _ = qb_ref.shape
    n_col = n_tile * rows_t
    bias = bias_ref[...]
    r_idx = lax.broadcasted_iota(jnp.int32, (page, page), 0)
    c_idx = lax.broadcasted_iota(jnp.int32, (page, page), 1)
    tri = (r_idx > c_idx).astype(BF16)

    def scores(kt_ref, valid):
        zs = [_dot(qb_ref[t], kt_ref[t * MXU_TILE:(t + 1) * MXU_TILE, :].astype(BF16)) for t in range(n_tile)]
        z = jnp.concatenate(zs, axis=0) + bias
        log_b, l1m = _log_beta_pair(z)
        if valid is not None:
            l1m = jnp.where(valid, l1m, 0.0)
        suf = _suffix_sum_lanes(l1m, tri)
        return log_b, suf, suf[:, 0:1] + l1m[:, 0:1]

    def weighted_values(log_b, suf, c_sum, vt_ref, valid):
        w = jnp.exp(log_b + suf + c_sum)
        if valid is not None:
            w = jnp.where(valid, w, 0.0)
        wb = w.astype(BF16)
        outs = [_dot_nt(wb[t * rows_t:(t + 1) * rows_t, :], vt_ref[t * MXU_TILE:(t + 1) * MXU_TILE, :].astype(BF16))
                for t in range(n_tile)]
        return jnp.concatenate(outs, axis=0)

    @pl.when(j == 0)
    def _():
        key_i = lax.broadcasted_iota(jnp.int32, (n_col, page), 1)
        qry_i = lax.broadcasted_iota(jnp.int32, (n_col, page), 0) % n_q
        valid = key_i < qry_i
        log_b, suf, tot = scores(knt_ref, valid)
        acc_ref[...] = weighted_values(log_b, suf, jnp.zeros((n_col, 1), F32), vnt_ref, valid)
        c_ref[...] = tot

    @pl.when(j > 0)
    def _():
        parts = [scores(k_refs[pp], None) for pp in range(n_pp)]
        c_sum = c_ref[...]
        acc = acc_ref[...]
        for pp in range(n_pp):
            log_b, suf, tot = parts[pp]
            acc = acc + weighted_values(log_b, suf, c_sum, v_refs[pp], None)
            c_sum = c_sum + tot
        acc_ref[...] = acc
        c_ref[...] = c_sum

    @pl.when(j == pl.num_programs(1) - 1)
    def _():
        row_h = lax.broadcasted_iota(jnp.int32, (rows_t, MXU_TILE), 0) // n_q
        col_h = lax.broadcasted_iota(jnp.int32, (rows_t, MXU_TILE), 1) // hd
        outs = []
        for t in range(n_tile):
            picked = jnp.where(row_h == col_h, acc_ref[t * rows_t:(t + 1) * rows_t, :], 0.0)
            out_t = picked[0:n_q, :]
            for g in range(1, rows_t // n_q):
                out_t = out_t + picked[g * n_q:(g + 1) * n_q, :]
            outs.append(out_t)
        o_ref[...] = jnp.concatenate(outs, axis=1).astype(o_ref.dtype)


def _sample_attention(q, k_new, v_new, cache_kt, cache_vt, page_table, bias, n_q, hd):
    n_rows, d = q.shape
    n_seq = n_rows // n_q
    n_pool, _, page = cache_kt.shape
    n_pages = page_table.shape[1]
    n_pp = 8 if n_pages % 8 == 0 else 1
    n_steps = n_pages // n_pp
    n_tile = d // MXU_TILE
    hpt = MXU_TILE // hd
    rows_t = hpt * n_q
    n_col = n_tile * rows_t
    q5 = q.reshape(n_seq, n_q, n_tile, hpt, hd).transpose(0, 2, 3, 1, 4)
    qb = q5[:, :, :, :, None, :] * jnp.eye(hpt, dtype=q.dtype)[None, None, :, None, :, None]
    qb = qb.reshape(n_seq, n_tile, rows_t, MXU_TILE)
    bias_rows = jnp.repeat(bias.astype(F32), n_q).reshape(n_col, 1)

    def new_page(a):
        a = jnp.pad(a.reshape(n_seq, n_q, d), ((0, 0), (0, page - n_q), (0, 0)))
        return a.transpose(0, 2, 1)

    seq3 = lambda r, c: pl.BlockSpec((None, r, c), lambda b, j, pt: (b, 0, 0))

    def page_spec(pp):
        def index_map(b, j, pt):
            group = jnp.maximum(j, 1) - 1
            return (pt[b, n_pages - 1 - group * n_pp - pp], 0, 0)
        return pl.BlockSpec((None, d, page), index_map)

    grid_spec = pltpu.PrefetchScalarGridSpec(
        num_scalar_prefetch=1,
        grid=(n_seq, n_steps + 1),
        in_specs=[pl.BlockSpec((None, n_tile, rows_t, MXU_TILE), lambda b, j, pt: (b, 0, 0, 0)),
                  pl.BlockSpec((n_col, 1), lambda b, j, pt: (0, 0)),
                  seq3(d, page), seq3(d, page)]
                 + [page_spec(pp) for pp in range(n_pp)] * 2,
        out_specs=seq3(n_q, d),
        scratch_shapes=[pltpu.VMEM((n_col, MXU_TILE), F32), pltpu.VMEM((n_col, 1), F32)],
    )
    out = pl.pallas_call(
        functools.partial(_sample_attn_kernel, n_pp=n_pp, page=page, n_q=n_q, hd=hd),
        grid_spec=grid_spec,
        out_shape=jax.ShapeDtypeStruct((n_seq, n_q, d), BF16),
        compiler_params=_params(("arbitrary", "arbitrary")),
    )(page_table, qb, bias_rows, new_page(k_new), new_page(v_new), *([cache_kt] * n_pp), *([cache_vt] * n_pp))
    return out.reshape(n_rows, d)


def _trunk(x3, mods, kv_mod, per_row, h0_re, h0_im, past, w):
    n_seq, t_len, d = x3.shape
    n_layers = len(mods)
    n_a = n_layers // 2
    hd = d // w["sb_bias"].shape[1]
    x = x3.reshape(n_seq * t_len, d)
    new_re, new_im = [], []
    k_new = v_new = None
    for l in range(n_layers):
        m = mods[l]
        g = w["norm_g"][l]
        if l == n_a:
            if past is None:
                k_new, v_new = _norm_proj_t(x, n_seq, t_len, kv_mod[0], kv_mod[1], w["kv_norm_g"], w["w_kv_t"])
            else:
                k_new, v_new = _norm_proj(x, t_len, kv_mod[0], kv_mod[1], w["kv_norm_g"], w["w_kv"],
                                          per_row=per_row, out_dtype=F32)
        ffn1 = (w["ffn_w_gate"][l][0], w["ffn_w_up"][l][0], w["ffn_w_down"][l][0])
        ffn2 = (w["ffn_w_gate"][l][1], w["ffn_w_up"][l][1], w["ffn_w_down"][l][1])
        if l < n_a:
            x, u = _ffn(x, t_len, m[0], m[1], m[2], g[0], *ffn1, per_row=per_row, post="mod",
                        post_args=(g[1], m[3], m[4]), post_time_major=not per_row)
            if per_row:
                u = u.reshape(n_seq, t_len, d).transpose(1, 0, 2)
            u_tm = u.reshape(t_len * n_seq, d)
            y_tm, s_re, s_im = _s5_scan(u_tm, n_seq, w["s5_tables"][l], w["ssm_d"][l], h0_re[l], h0_im[l])
            if per_row:
                y = y_tm.reshape(t_len, n_seq, d).transpose(1, 0, 2).reshape(n_seq * t_len, d)
            else:
                y = y_tm.reshape(t_len, n_seq * d)
            x = _glu_residual(y, x, t_len, m[5], w["glu_w"][l], w["glu_b"][l], per_row=per_row,
                              y_time_major=not per_row)
            new_re.append(s_re)
            new_im.append(s_im)
        else:
            jb = l - n_a
            (x,) = _ffn(x, t_len, m[0], m[1], m[2], g[0], *ffn1, per_row=per_row)
            (q,) = _norm_proj(x, t_len, m[3], m[4], g[1], w["w_q"][jb], per_row=per_row, out_dtype=BF16,
                              out_scale=hd ** -0.5)
            if past is None:
                o = _prompt_attention(q, k_new, v_new, w["sb_bias"][jb], n_seq, t_len, hd)
            else:
                o = _sample_attention(q, k_new, v_new, past[0], past[1], past[2], w["sb_bias"][jb], t_len, hd)
            x = _oproj_residual(o, x, t_len, m[5], w["w_o"][jb], per_row=per_row)
        if l == n_layers - 1:
            (x,) = _ffn(x, t_len, m[6], m[7], m[8], g[2], *ffn2, per_row=per_row, post="plain",
                        post_args=(w["norm_f"],), emit_x=False)
        else:
            (x,) = _ffn(x, t_len, m[6], m[7], m[8], g[2], *ffn2, per_row=per_row)
    return x, k_new, v_new, new_re, new_im


def kernel(x_prompt, x_sample, cache_k, cache_v, state_ssm_re, state_ssm_im, page_table,
           c_prompt, c_sample, ada_w, ada_b, norm_g, ffn_w_gate, ffn_w_up, ffn_w_down,
           ssm_a_re, ssm_a_im, ssm_log_dt, ssm_b_re, ssm_b_im, ssm_c_re, ssm_c_im, ssm_d,
           glu_w, glu_b, kv_ada_w, kv_ada_b, kv_norm_g, w_kv, w_q, w_o, sb_bias, norm_f):
    n_p, t_p, d = x_prompt.shape
    n_s, t_s, _ = x_sample.shape
    n_layers = ada_w.shape[0]
    n_a = n_layers // 2
    n_heads = sb_bias.shape[1]
    hd = d // n_heads
    n_pool, page = cache_k.shape[:2]

    c_all = jnp.concatenate([c_prompt, c_sample], axis=0)
    mod_all = _ada_vectors(c_all, ada_w, ada_b)
    kv_all = _ada_vectors(c_all, kv_ada_w[None], kv_ada_b[None])[0]

    def split(vecs, n_vec):
        pr = [vecs[:n_p, k * d:(k + 1) * d].reshape(n_p, 1, d) for k in range(n_vec)]
        sa = [jnp.repeat(vecs[n_p:, k * d:(k + 1) * d], t_s, axis=0) for k in range(n_vec)]
        return pr, sa

    mods_p, mods_s = zip(*[split(mod_all[l], N_MOD) for l in range(n_layers)])
    kv_p, kv_s = split(kv_all, 2)

    w = dict(
        norm_g=norm_g, kv_norm_g=kv_norm_g, norm_f=norm_f, ssm_d=ssm_d, glu_b=glu_b, sb_bias=sb_bias,
        ffn_w_gate=ffn_w_gate.astype(BF16), ffn_w_up=ffn_w_up.astype(BF16), ffn_w_down=ffn_w_down.astype(BF16),
        glu_w=glu_w.astype(BF16), w_kv=w_kv.astype(BF16), w_kv_t=w_kv.T.astype(BF16),
        w_q=w_q.astype(BF16), w_o=w_o.astype(BF16),
        s5_tables=[_s5_tables(ssm_a_re[l], ssm_a_im[l], ssm_log_dt[l], ssm_b_re[l], ssm_b_im[l],
                              ssm_c_re[l], ssm_c_im[l]) for l in range(n_a)],
    )

    zeros = jnp.zeros((n_a, n_p, state_ssm_re.shape[2] * state_ssm_re.shape[3]), F32)
    y_p, k_p, v_p, re_p, im_p = _trunk(x_prompt, mods_p, kv_p, False, zeros, zeros, None, w)

    h0_re = state_ssm_re.reshape(n_a, n_s, -1)
    h0_im = state_ssm_im.reshape(n_a, n_s, -1)
    to_cm = lambda c: c.transpose(0, 2, 3, 1).reshape(n_pool, d, page)
    past = (to_cm(cache_k), to_cm(cache_v), page_table)
    y_s, k_s, v_s, re_s, im_s = _trunk(x_sample, mods_s, kv_s, True, h0_re, h0_im, past, w)

    from_cm = lambda a: a.reshape(n_p, n_heads, hd, t_p).transpose(0, 3, 1, 2)
    st_shape_p = (n_a, n_p) + state_ssm_re.shape[2:]
    st_shape_s = (n_a, n_s) + state_ssm_re.shape[2:]
    return (y_p.reshape(n_p, t_p, d), y_s.reshape(n_s, t_s, d),
            from_cm(k_p), from_cm(v_p),
            k_s.reshape(n_s, t_s, n_heads, hd), v_s.reshape(n_s, t_s, n_heads, hd),
            jnp.stack(re_p).reshape(st_shape_p), jnp.stack(im_p).reshape(st_shape_p),
            jnp.stack(re_s).reshape(st_shape_s), jnp.stack(im_s).reshape(st_shape_s))
```

```python
import functools

import jax
import jax.numpy as jnp
from jax import lax
from jax.experimental import pallas as pl
from jax.experimental.pallas import tpu as pltpu

EPS = 1e-6
GROUP_CH = 16
N_MOD = 9
LANES = 128
MXU_TILE = 256
HEADS_PER_STAGE = 2
V7X_VMEM_LIMIT = 56 * 1024 * 1024

F32 = jnp.float32
BF16 = jnp.bfloat16


def _dot(a, b):
    return jnp.dot(a, b, preferred_element_type=F32)


def _dot_nt(a, b):
    return lax.dot_general(a, b, (((1,), (1,)), ((), ())), preferred_element_type=F32)


def _split_bf16(x):
    hi = x.astype(BF16)
    lo = (x - hi.astype(F32)).astype(BF16)
    return hi, lo


def _mod_norm(x, g, shift, scale):
    xn = x * lax.rsqrt(jnp.mean(x * x, axis=-1, keepdims=True) + EPS)
    return (xn * g) * (1.0 + scale) + shift


def _params(sem):
    return pltpu.CompilerParams(dimension_semantics=sem, vmem_limit_bytes=V7X_VMEM_LIMIT)


def _resident(shape, index_map):
    return pl.BlockSpec(shape, index_map, pipeline_mode=pl.Buffered(1))


def _ada_kernel(c_ref, w_ref, b_ref, o_ref):
    c = c_ref[...]
    sc_hi, sc_lo = _split_bf16(c * jax.nn.sigmoid(c))
    w_hi, w_lo = _split_bf16(w_ref[...])
    o_ref[...] = _dot(sc_hi, w_hi) + (_dot(sc_hi, w_lo) + _dot(sc_lo, w_hi)) + b_ref[...]


def _ada_vectors(c, w, b):
    n_l, d, n = w.shape
    m = c.shape[0]
    tn = max(t for t in range(LANES, min(n, 1024) + 1, LANES) if n % t == 0)
    return pl.pallas_call(
        _ada_kernel,
        grid=(n_l, n // tn),
        in_specs=[pl.BlockSpec((m, d), lambda l, j: (0, 0)),
                  pl.BlockSpec((None, d, tn), lambda l, j: (l, 0, j)),
                  pl.BlockSpec((None, 1, tn), lambda l, j: (l, 0, j))],
        out_specs=pl.BlockSpec((None, m, tn), lambda l, j: (l, 0, j)),
        out_shape=jax.ShapeDtypeStruct((n_l, m, n), F32),
        compiler_params=_params(("arbitrary", "arbitrary")),
    )(c, w, b.reshape(n_l, 1, n))


def _row_specs(n_rows, seq_len, d, tm, per_row):
    x_spec = pl.BlockSpec((tm, d), lambda i: (i, 0))
    if per_row:
        return x_spec, x_spec
    blocks_per_seq = seq_len // tm
    return x_spec, pl.BlockSpec((None, 1, d), lambda i: (i // blocks_per_seq, 0, 0))


def _pick_tm(seq_len, n_rows, per_row, cap):
    if per_row:
        return n_rows
    tm = min(cap, seq_len)
    assert seq_len % tm == 0
    return tm


def _ffn_kernel(*refs, n_chunks, fc, post, emit_x):
    x_ref, sh_ref, sc_ref, gt_ref, g_ref, wg_ref, wu_ref, wd_ref = refs[:8]
    pos = 8
    if post == "mod":
        pg_ref, psh_ref, psc_ref = refs[pos:pos + 3]
        pos += 3
    elif post == "plain":
        pg_ref = refs[pos]
        pos += 1
    outs = list(refs[pos:-1])
    acc_ref = refs[-1]

    x = x_ref[...]
    h = _mod_norm(x, g_ref[...], sh_ref[...], sc_ref[...]).astype(BF16)
    for c in range(n_chunks):
        lo = c * fc
        gate = _dot(h, wg_ref[:, lo:lo + fc])
        up = _dot(h, wu_ref[:, lo:lo + fc])
        a = ((gate * jax.nn.sigmoid(gate)) * up).astype(BF16)
        part = _dot(a, wd_ref[lo:lo + fc, :])
        if c == 0:
            acc_ref[...] = part
        else:
            acc_ref[...] += part
    x_new = x + (0.5 * gt_ref[...]) * acc_ref[...]
    if emit_x:
        outs.pop(0)[...] = x_new
    if post == "mod":
        outs.pop(0)[...] = _mod_norm(x_new, pg_ref[...], psh_ref[...], psc_ref[...])
    elif post == "plain":
        xn = x_new * lax.rsqrt(jnp.mean(x_new * x_new, axis=-1, keepdims=True) + EPS)
        outs.pop(0)[...] = xn * pg_ref[...]


def _ffn(x, seq_len, shift, scale, gate, g, wg, wu, wd, *, per_row, post=None, post_args=(), emit_x=True):
    n_rows, d = x.shape
    f = wg.shape[1]
    tm = _pick_tm(seq_len, n_rows, per_row, 512)
    fc = 256 if f % 256 == 0 else f
    x_spec, v_spec = _row_specs(n_rows, seq_len, d, tm, per_row)
    g_spec = pl.BlockSpec((1, d), lambda i: (0, 0))
    in_specs = [x_spec, v_spec, v_spec, v_spec, g_spec,
                _resident((d, f), lambda i: (0, 0)), _resident((d, f), lambda i: (0, 0)),
                _resident((f, d), lambda i: (0, 0))]
    args = [x, shift, scale, gate, g.reshape(1, d), wg, wu, wd]
    if post == "mod":
        in_specs += [g_spec, v_spec, v_spec]
        args += [post_args[0].reshape(1, d), post_args[1], post_args[2]]
    elif post == "plain":
        in_specs += [g_spec]
        args += [post_args[0].reshape(1, d)]
    out_specs, out_shape = [], []
    if emit_x:
        out_specs.append(x_spec)
        out_shape.append(jax.ShapeDtypeStruct((n_rows, d), F32))
    if post is not None:
        out_specs.append(x_spec)
        out_shape.append(jax.ShapeDtypeStruct((n_rows, d), F32))
    outs = pl.pallas_call(
        functools.partial(_ffn_kernel, n_chunks=f // fc, fc=fc, post=post, emit_x=emit_x),
        grid=(n_rows // tm,),
        in_specs=in_specs,
        out_specs=out_specs,
        out_shape=out_shape,
        scratch_shapes=[pltpu.VMEM((tm, d), F32)],
        compiler_params=_params(("arbitrary",)),
    )(*args)
    return outs


def _ssm_kernel(u_ref, bmat_ref, cmat_ref, are_ref, aim_ref, d_ref, h0re_ref, h0im_ref,
                y_ref, sre_ref, sim_ref, hs_ref, hre_ref, him_ref, *, n_b, tt, n_half, n_blk, pitch):
    n_slab = n_half // LANES
    j = pl.program_id(0)

    @pl.when(j == 0)
    def _():
        hre_ref[...] = h0re_ref[...]
        him_ref[...] = h0im_ref[...]

    def channel_block(k, _):
        c0 = pl.multiple_of(k * LANES, LANES)
        s0 = pl.multiple_of(k * n_half, n_half)
        u = jnp.concatenate([u_ref[b, :, pl.ds(c0, LANES)] for b in range(n_b)], axis=0)
        bu = _dot(u.astype(BF16), bmat_ref[k])
        for s in range(2 * n_slab):
            for b in range(n_b):
                hs_ref[s, b * pitch:b * pitch + tt, :] = bu[b * tt:(b + 1) * tt, s * LANES:(s + 1) * LANES]
        a_re = [jnp.broadcast_to(are_ref[k, :, s * LANES:(s + 1) * LANES], (n_b, LANES)) for s in range(n_slab)]
        a_im = [jnp.broadcast_to(aim_ref[k, :, s * LANES:(s + 1) * LANES], (n_b, LANES)) for s in range(n_slab)]

        def step(t, carry):
            h_re, h_im = carry
            new_re, new_im = [], []
            for s in range(n_slab):
                rows = pl.ds(t, n_b, stride=pitch)
                nr = (a_re[s] * h_re[s] - a_im[s] * h_im[s]) + hs_ref[s, rows, :]
                ni = (a_re[s] * h_im[s] + a_im[s] * h_re[s]) + hs_ref[n_slab + s, rows, :]
                hs_ref[s, rows, :] = nr
                hs_ref[n_slab + s, rows, :] = ni
                new_re.append(nr)
                new_im.append(ni)
            return tuple(new_re), tuple(new_im)

        h0 = (tuple(hre_ref[:, pl.ds(s0 + s * LANES, LANES)] for s in range(n_slab)),
              tuple(him_ref[:, pl.ds(s0 + s * LANES, LANES)] for s in range(n_slab)))
        h_re, h_im = lax.fori_loop(0, tt, step, h0, unroll=2)
        for s in range(n_slab):
            hre_ref[:, pl.ds(s0 + s * LANES, LANES)] = h_re[s]
            him_ref[:, pl.ds(s0 + s * LANES, LANES)] = h_im[s]
        hs = jnp.concatenate(
            [jnp.concatenate([hs_ref[s, b * pitch:b * pitch + tt, :] for s in range(2 * n_slab)], axis=1)
             for b in range(n_b)], axis=0)
        y = _dot(hs.astype(BF16), cmat_ref[k]) + d_ref[:, pl.ds(c0, LANES)] * u
        for b in range(n_b):
            y_ref[b, :, pl.ds(c0, LANES)] = y[b * tt:(b + 1) * tt, :]
        return 0

    lax.fori_loop(0, n_blk, channel_block, 0)

    @pl.when(j == pl.num_programs(0) - 1)
    def _():
        sre_ref[...] = hre_ref[...]
        sim_ref[...] = him_ref[...]


def _s5_tables(a_re, a_im, log_dt, b_re, b_im, c_re, c_im):
    n_g, n_s = a_re.shape
    gpb = LANES // GROUP_CH
    n_blk = n_g // gpb
    dt = jnp.exp(log_dt)[:, None]
    mag = jnp.exp(a_re * dt)
    abar_re = mag * jnp.cos(a_im * dt)
    abar_im = mag * jnp.sin(a_im * dt)
    den = a_re * a_re + a_im * a_im
    xr, xi = abar_re - 1.0, abar_im
    q_re = (xr * a_re + xi * a_im) / den
    q_im = (xi * a_re - xr * a_im) / den
    bb_re = q_re[:, :, None] * b_re - q_im[:, :, None] * b_im
    bb_im = q_re[:, :, None] * b_im + q_im[:, :, None] * b_re
    eye = jnp.eye(gpb, dtype=F32)

    def b_table(bb):
        t = bb.reshape(n_blk, gpb, n_s, GROUP_CH).transpose(0, 1, 3, 2)
        t = t[:, :, :, None, :] * eye[None, :, None, :, None]
        return t.reshape(n_blk, gpb * GROUP_CH, gpb * n_s)

    def c_table(cc):
        t = cc.reshape(n_blk, gpb, GROUP_CH, n_s).transpose(0, 1, 3, 2)
        t = t[:, :, :, None, :] * eye[None, :, None, :, None]
        return t.reshape(n_blk, gpb * n_s, gpb * GROUP_CH)

    bmat = jnp.concatenate([b_table(bb_re), b_table(bb_im)], axis=2).astype(BF16)
    cmat = jnp.concatenate([c_table(c_re), c_table(-c_im)], axis=1).astype(BF16)
    n_half = gpb * n_s
    return (bmat, cmat, abar_re.reshape(n_blk, 1, n_half), abar_im.reshape(n_blk, 1, n_half))


def _s5_scan(u, n_b, tables, d_skip, h0_re, h0_im):
    bmat, cmat, abar_re, abar_im = tables
    n_rows, d = u.shape
    t_len = n_rows // n_b
    n_blk, _, n_half = abar_re.shape
    n_state = n_blk * n_half
    tt = min(32, t_len)
    assert t_len % tt == 0
    nt = t_len // tt
    pitch = tt + 8
    seq_spec = pl.BlockSpec((n_b, tt, d), lambda j: (0, j, 0))
    full = lambda shape: _resident(shape, lambda j: (0,) * len(shape))
    y, s_re, s_im = pl.pallas_call(
        functools.partial(_ssm_kernel, n_b=n_b, tt=tt, n_half=n_half, n_blk=n_blk, pitch=pitch),
        grid=(nt,),
        in_specs=[seq_spec, full(bmat.shape), full(cmat.shape), full(abar_re.shape), full(abar_im.shape),
                  full((1, d)), full((n_b, n_state)), full((n_b, n_state))],
        out_specs=[seq_spec] + [pl.BlockSpec((n_b, n_state), lambda j: (0, 0))] * 2,
        out_shape=[jax.ShapeDtypeStruct((n_b, t_len, d), F32)] + [jax.ShapeDtypeStruct((n_b, n_state), F32)] * 2,
        scratch_shapes=[pltpu.VMEM((2 * n_half // LANES, n_b * pitch, LANES), F32),
                        pltpu.VMEM((n_b, n_state), F32), pltpu.VMEM((n_b, n_state), F32)],
        compiler_params=_params(("arbitrary",)),
    )(u.reshape(n_b, t_len, d), bmat, cmat, abar_re, abar_im, d_skip.reshape(1, d), h0_re, h0_im)
    return y.reshape(n_rows, d), s_re, s_im


def _glu_kernel(y_ref, x_ref, gt_ref, w_ref, b_ref, o_ref):
    d = x_ref.shape[-1]
    ge = jax.nn.gelu(y_ref[...]).astype(BF16)
    gl = _dot(ge, w_ref[...]) + b_ref[...]
    mix = gl[:, :d] * jax.nn.sigmoid(gl[:, d:])
    o_ref[...] = x_ref[...] + gt_ref[...] * mix


def _glu_residual(y, x, seq_len, gate, w, b, *, per_row):
    n_rows, d = x.shape
    tm = _pick_tm(seq_len, n_rows, per_row, 512)
    x_spec, v_spec = _row_specs(n_rows, seq_len, d, tm, per_row)
    return pl.pallas_call(
        _glu_kernel,
        grid=(n_rows // tm,),
        in_specs=[x_spec, x_spec, v_spec, _resident((d, 2 * d), lambda i: (0, 0)),
                  pl.BlockSpec((1, 2 * d), lambda i: (0, 0))],
        out_specs=x_spec,
        out_shape=jax.ShapeDtypeStruct((n_rows, d), F32),
        compiler_params=_params(("arbitrary",)),
    )(y, x, gate, w, b.reshape(1, 2 * d))


def _proj_kernel(x_ref, sh_ref, sc_ref, g_ref, w_ref, *o_refs, out_scale):
    d = x_ref.shape[-1]
    h = _mod_norm(x_ref[...], g_ref[...], sh_ref[...], sc_ref[...]).astype(BF16)
    for k, o_ref in enumerate(o_refs):
        o_ref[...] = (_dot(h, w_ref[:, k * d:(k + 1) * d]) * out_scale).astype(o_ref.dtype)


def _norm_proj(x, seq_len, shift, scale, g, w, *, per_row, out_dtype, out_scale=1.0):
    n_rows, d = x.shape
    n_out = w.shape[1] // d
    tm = _pick_tm(seq_len, n_rows, per_row, 512)
    x_spec, v_spec = _row_specs(n_rows, seq_len, d, tm, per_row)
    return pl.pallas_call(
        functools.partial(_proj_kernel, out_scale=out_scale),
        grid=(n_rows // tm,),
        in_specs=[x_spec, v_spec, v_spec, pl.BlockSpec((1, d), lambda i: (0, 0)),
                  _resident((d, n_out * d), lambda i: (0, 0))],
        out_specs=[x_spec] * n_out,
        out_shape=[jax.ShapeDtypeStruct((n_rows, d), out_dtype)] * n_out,
        compiler_params=_params(("arbitrary",)),
    )(x, shift, scale, g.reshape(1, d), w)


def _proj_t_kernel(x_ref, sh_ref, sc_ref, g_ref, wt_ref, *o_refs):
    d = x_ref.shape[-1]
    h = _mod_norm(x_ref[...], g_ref[...], sh_ref[...], sc_ref[...]).astype(BF16)
    for k, o_ref in enumerate(o_refs):
        o_ref[...] = _dot_nt(wt_ref[k * d:(k + 1) * d, :], h)


def _norm_proj_t(x, n_seq, seq_len, shift, scale, g, wt):
    n_rows, d = x.shape
    n_out = wt.shape[0] // d
    tm = _pick_tm(seq_len, n_rows, False, 512)
    nt = seq_len // tm
    x_spec, v_spec = _row_specs(n_rows, seq_len, d, tm, False)
    o_spec = pl.BlockSpec((None, d, tm), lambda i: (i // nt, 0, i % nt))
    return pl.pallas_call(
        _proj_t_kernel,
        grid=(n_rows // tm,),
        in_specs=[x_spec, v_spec, v_spec, pl.BlockSpec((1, d), lambda i: (0, 0)),
                  _resident((n_out * d, d), lambda i: (0, 0))],
        out_specs=[o_spec] * n_out,
        out_shape=[jax.ShapeDtypeStruct((n_seq, d, seq_len), F32)] * n_out,
        compiler_params=_params(("arbitrary",)),
    )(x, shift, scale, g.reshape(1, d), wt)


def _oproj_kernel(o_ref, x_ref, gt_ref, w_ref, out_ref):
    out_ref[...] = x_ref[...] + gt_ref[...] * _dot(o_ref[...], w_ref[...])


def _oproj_residual(o, x, seq_len, gate, w, *, per_row):
    n_rows, d = x.shape
    tm = _pick_tm(seq_len, n_rows, per_row, 512)
    x_spec, v_spec = _row_specs(n_rows, seq_len, d, tm, per_row)
    return pl.pallas_call(
        _oproj_kernel,
        grid=(n_rows // tm,),
        in_specs=[x_spec, x_spec, v_spec, _resident((d, d), lambda i: (0, 0))],
        out_specs=x_spec,
        out_shape=jax.ShapeDtypeStruct((n_rows, d), F32),
        compiler_params=_params(("arbitrary",)),
    )(o, x, gate, w)


def _log_beta_pair(z):
    sp = jnp.log(1.0 + jnp.exp(-jnp.abs(z)))
    return jnp.minimum(z, 0.0) - sp, -jnp.maximum(z, 0.0) - sp


def _suffix_sum_lanes(l1m, tri):
    hi, lo = _split_bf16(l1m)
    return _dot(hi, tri) + _dot(lo, tri)


def _prompt_attn_kernel(bias_ref, q_ref, kt_ref, vt_ref, o_ref, ktb_ref, vtm_ref, *, tq, hd):
    grp = pl.program_id(1)
    i = pl.program_id(2)
    n_hh = MXU_TILE // hd

    @pl.when(i == 0)
    def _():
        ktb_ref[...] = kt_ref[...].astype(BF16)
        vt = vt_ref[...]
        ch_head = lax.broadcasted_iota(jnp.int32, vt.shape, 0) // hd
        for hh in range(n_hh):
            vtm_ref[hh] = jnp.where(ch_head == hh, vt, 0.0).astype(BF16)

    q2 = q_ref[...]
    lane_head = lax.broadcasted_iota(jnp.int32, (tq, MXU_TILE), 1) // hd
    r_idx = lax.broadcasted_iota(jnp.int32, (tq, tq), 0)
    c_idx = lax.broadcasted_iota(jnp.int32, (tq, tq), 1)
    tri = (r_idx > c_idx).astype(BF16)
    valid = c_idx < r_idx
    qns = [jnp.where(lane_head == hh, -q2, jnp.zeros_like(q2)) for hh in range(n_hh)]
    biases = [bias_ref[grp * n_hh + hh] for hh in range(n_hh)]
    heads = range(n_hh)

    def block(j, carry, masked):
        c_sums, acc = carry
        c0 = pl.multiple_of(j * tq, tq)
        kt = ktb_ref[:, pl.ds(c0, tq)]
        new_sums = []
        for h0 in range(0, n_hh, HEADS_PER_STAGE):
            hs = range(h0, min(h0 + HEADS_PER_STAGE, n_hh))
            zns = {hh: _dot(qns[hh], kt) - biases[hh] for hh in hs}
            l1bs, log_bs, firsts = {}, {}, {}
            for hh in hs:
                neg_abs = pltpu.bitcast(pltpu.bitcast(zns[hh], jnp.uint32) | jnp.uint32(0x80000000), F32)
                l1m = jnp.minimum(zns[hh], 0.0) - jnp.log(1.0 + jnp.exp(neg_abs))
                if masked:
                    l1m = jnp.where(valid, l1m, 0.0)
                log_bs[hh] = l1m - zns[hh]
                firsts[hh] = l1m[:, 0:1]
                l1bs[hh] = l1m.astype(BF16)
            sufs = {hh: _dot(l1bs[hh], tri) for hh in hs}
            ws = {}
            for hh in hs:
                w = jnp.exp(log_bs[hh] + sufs[hh] + c_sums[hh])
                ws[hh] = (jnp.where(valid, w, 0.0) if masked else w).astype(BF16)
            for hh in hs:
                acc = acc + _dot_nt(ws[hh], vtm_ref[hh, :, pl.ds(c0, tq)])
                new_sums.append(c_sums[hh] + (sufs[hh][:, 0:1] + firsts[hh]))
        return tuple(new_sums), acc

    carry = (tuple(jnp.zeros((tq, 1), F32) for _ in range(n_hh)), jnp.zeros((tq, MXU_TILE), F32))
    carry = block(i, carry, True)
    carry = lax.fori_loop(0, i, lambda jj, cr: block(i - 1 - jj, cr, False), carry)
    o_ref[...] = carry[1].astype(o_ref.dtype)


def _prompt_attention(q, kt, vt, bias, n_seq, seq_len, hd):
    n_rows, d = q.shape
    tq = min(256, seq_len)
    nq = seq_len // tq
    n_grp = d // MXU_TILE
    kv_spec = pl.BlockSpec((None, MXU_TILE, seq_len), lambda b, p, i: (b, p, 0))
    return pl.pallas_call(
        functools.partial(_prompt_attn_kernel, tq=tq, hd=hd),
        grid=(n_seq, n_grp, nq),
        in_specs=[pl.BlockSpec(memory_space=pltpu.SMEM),
                  pl.BlockSpec((tq, MXU_TILE), lambda b, p, i: (b * nq + i, p)),
                  kv_spec, kv_spec],
        out_specs=pl.BlockSpec((tq, MXU_TILE), lambda b, p, i: (b * nq + i, p)),
        out_shape=jax.ShapeDtypeStruct((n_rows, d), BF16),
        scratch_shapes=[pltpu.VMEM((MXU_TILE, seq_len), BF16),
                        pltpu.VMEM((MXU_TILE // hd, MXU_TILE, seq_len), BF16)],
        compiler_params=_params(("arbitrary", "arbitrary", "arbitrary")),
    )(bias, q, kt, vt)


def _sample_attn_kernel(pt_ref, qb_ref, bias_ref, knt_ref, vnt_ref, *rest, n_pp, page, n_q, hd):
    k_refs = rest[:n_pp]
    v_refs = rest[n_pp:2 * n_pp]
    o_ref, acc_ref, c_ref = rest[2 * n_pp:]
    j = pl.program_id(1)
    n_tile, rows_t, _ = qb_ref.shape
    n_col = n_tile * rows_t
    bias = bias_ref[...]
    r_idx = lax.broadcasted_iota(jnp.int32, (page, page), 0)
    c_idx = lax.broadcasted_iota(jnp.int32, (page, page), 1)
    tri = (r_idx > c_idx).astype(BF16)

    def scores(kt_ref, valid):
        zs = [_dot(qb_ref[t], kt_ref[t * MXU_TILE:(t + 1) * MXU_TILE, :].astype(BF16)) for t in range(n_tile)]
        z = jnp.concatenate(zs, axis=0) + bias
        log_b, l1m = _log_beta_pair(z)
        if valid is not None:
            l1m = jnp.where(valid, l1m, 0.0)
        suf = _suffix_sum_lanes(l1m, tri)
        return log_b, suf, suf[:, 0:1] + l1m[:, 0:1]

    def weighted_values(log_b, suf, c_sum, vt_ref, valid):
        w = jnp.exp(log_b + suf + c_sum)
        if valid is not None:
            w = jnp.where(valid, w, 0.0)
        wb = w.astype(BF16)
        outs = [_dot_nt(wb[t * rows_t:(t + 1) * rows_t, :], vt_ref[t * MXU_TILE:(t + 1) * MXU_TILE, :].astype(BF16))
                for t in range(n_tile)]
        return jnp.concatenate(outs, axis=0)

    @pl.when(j == 0)
    def _():
        key_i = lax.broadcasted_iota(jnp.int32, (n_col, page), 1)
        qry_i = lax.broadcasted_iota(jnp.int32, (n_col, page), 0) % n_q
        valid = key_i < qry_i
        log_b, suf, tot = scores(knt_ref, valid)
        acc_ref[...] = weighted_values(log_b, suf, jnp.zeros((n_col, 1), F32), vnt_ref, valid)
        c_ref[...] = tot

    @pl.when(j > 0)
    def _():
        parts = [scores(k_refs[pp], None) for pp in range(n_pp)]
        c_sum = c_ref[...]
        acc = acc_ref[...]
        for pp in range(n_pp):
            log_b, suf, tot = parts[pp]
            acc = acc + weighted_values(log_b, suf, c_sum, v_refs[pp], None)
            c_sum = c_sum + tot
        acc_ref[...] = acc
        c_ref[...] = c_sum

    @pl.when(j == pl.num_programs(1) - 1)
    def _():
        row_h = lax.broadcasted_iota(jnp.int32, (rows_t, MXU_TILE), 0) // n_q
        col_h = lax.broadcasted_iota(jnp.int32, (rows_t, MXU_TILE), 1) // hd
        outs = []
        for t in range(n_tile):
            picked = jnp.where(row_h == col_h, acc_ref[t * rows_t:(t + 1) * rows_t, :], 0.0)
            out_t = picked[0:n_q, :]
            for g in range(1, rows_t // n_q):
                out_t = out_t + picked[g * n_q:(g + 1) * n_q, :]
            outs.append(out_t)
        o_ref[...] = jnp.concatenate(outs, axis=1).astype(o_ref.dtype)


def _sample_attention(q, k_new, v_new, cache_kt, cache_vt, page_table, bias, n_q, hd):
    n_rows, d = q.shape
    n_seq = n_rows // n_q
    n_pool, _, page = cache_kt.shape
    n_pages = page_table.shape[1]
    n_pp = 8 if n_pages % 8 == 0 else 1
    n_steps = n_pages // n_pp
    n_tile = d // MXU_TILE
    hpt = MXU_TILE // hd
    rows_t = hpt * n_q
    n_col = n_tile * rows_t
    q5 = q.reshape(n_seq, n_q, n_tile, hpt, hd).transpose(0, 2, 3, 1, 4)
    qb = q5[:, :, :, :, None, :] * jnp.eye(hpt, dtype=q.dtype)[None, None, :, None, :, None]
    qb = qb.reshape(n_seq, n_tile, rows_t, MXU_TILE)
    bias_rows = jnp.repeat(bias.astype(F32), n_q).reshape(n_col, 1)

    def new_page(a):
        a = jnp.pad(a.reshape(n_seq, n_q, d), ((0, 0), (0, page - n_q), (0, 0)))
        return a.transpose(0, 2, 1)

    seq3 = lambda r, c: pl.BlockSpec((None, r, c), lambda b, j, pt: (b, 0, 0))

    def page_spec(pp):
        def index_map(b, j, pt):
            group = jnp.maximum(j, 1) - 1
            return (pt[b, n_pages - 1 - group * n_pp - pp], 0, 0)
        return pl.BlockSpec((None, d, page), index_map)

    grid_spec = pltpu.PrefetchScalarGridSpec(
        num_scalar_prefetch=1,
        grid=(n_seq, n_steps + 1),
        in_specs=[pl.BlockSpec((None, n_tile, rows_t, MXU_TILE), lambda b, j, pt: (b, 0, 0, 0)),
                  pl.BlockSpec((n_col, 1), lambda b, j, pt: (0, 0)),
                  seq3(d, page), seq3(d, page)]
                 + [page_spec(pp) for pp in range(n_pp)] * 2,
        out_specs=seq3(n_q, d),
        scratch_shapes=[pltpu.VMEM((n_col, MXU_TILE), F32), pltpu.VMEM((n_col, 1), F32)],
    )
    out = pl.pallas_call(
        functools.partial(_sample_attn_kernel, n_pp=n_pp, page=page, n_q=n_q, hd=hd),
        grid_spec=grid_spec,
        out_shape=jax.ShapeDtypeStruct((n_seq, n_q, d), BF16),
        compiler_params=_params(("arbitrary", "arbitrary")),
    )(page_table, qb, bias_rows, new_page(k_new), new_page(v_new), *([cache_kt] * n_pp), *([cache_vt] * n_pp))
    return out.reshape(n_rows, d)


def _trunk(x3, mods, kv_mod, per_row, h0_re, h0_im, past, w):
    n_seq, t_len, d = x3.shape
    n_layers = len(mods)
    n_a = n_layers // 2
    hd = d // w["sb_bias"].shape[1]
    x = x3.reshape(n_seq * t_len, d)
    new_re, new_im = [], []
    k_new = v_new = None
    for l in range(n_layers):
        m = mods[l]
        g = w["norm_g"][l]
        if l == n_a:
            if past is None:
                k_new, v_new = _norm_proj_t(x, n_seq, t_len, kv_mod[0], kv_mod[1], w["kv_norm_g"], w["w_kv_t"])
            else:
                k_new, v_new = _norm_proj(x, t_len, kv_mod[0], kv_mod[1], w["kv_norm_g"], w["w_kv"],
                                          per_row=per_row, out_dtype=F32)
        ffn1 = (w["ffn_w_gate"][l][0], w["ffn_w_up"][l][0], w["ffn_w_down"][l][0])
        ffn2 = (w["ffn_w_gate"][l][1], w["ffn_w_up"][l][1], w["ffn_w_down"][l][1])
        if l < n_a:
            x, u = _ffn(x, t_len, m[0], m[1], m[2], g[0], *ffn1, per_row=per_row, post="mod",
                        post_args=(g[1], m[3], m[4]))
            y, s_re, s_im = _s5_scan(u, n_seq, w["s5_tables"][l], w["ssm_d"][l], h0_re[l], h0_im[l])
            x = _glu_residual(y, x, t_len, m[5], w["glu_w"][l], w["glu_b"][l], per_row=per_row)
            new_re.append(s_re)
            new_im.append(s_im)
        else:
            jb = l - n_a
            (x,) = _ffn(x, t_len, m[0], m[1], m[2], g[0], *ffn1, per_row=per_row)
            (q,) = _norm_proj(x, t_len, m[3], m[4], g[1], w["w_q"][jb], per_row=per_row, out_dtype=BF16,
                              out_scale=hd ** -0.5)
            if past is None:
                o = _prompt_attention(q, k_new, v_new, w["sb_bias"][jb], n_seq, t_len, hd)
            else:
                o = _sample_attention(q, k_new, v_new, past[0], past[1], past[2], w["sb_bias"][jb], t_len, hd)
            x = _oproj_residual(o, x, t_len, m[5], w["w_o"][jb], per_row=per_row)
        if l == n_layers - 1:
            (x,) = _ffn(x, t_len, m[6], m[7], m[8], g[2], *ffn2, per_row=per_row, post="plain",
                        post_args=(w["norm_f"],), emit_x=False)
        else:
            (x,) = _ffn(x, t_len, m[6], m[7], m[8], g[2], *ffn2, per_row=per_row)
    return x, k_new, v_new, new_re, new_im


def kernel(x_prompt, x_sample, cache_k, cache_v, state_ssm_re, state_ssm_im, page_table,
           c_prompt, c_sample, ada_w, ada_b, norm_g, ffn_w_gate, ffn_w_up, ffn_w_down,
           ssm_a_re, ssm_a_im, ssm_log_dt, ssm_b_re, ssm_b_im, ssm_c_re, ssm_c_im, ssm_d,
           glu_w, glu_b, kv_ada_w, kv_ada_b, kv_norm_g, w_kv, w_q, w_o, sb_bias, norm_f):
    n_p, t_p, d = x_prompt.shape
    n_s, t_s, _ = x_sample.shape
    n_layers = ada_w.shape[0]
    n_a = n_layers // 2
    n_heads = sb_bias.shape[1]
    hd = d // n_heads
    n_pool, page = cache_k.shape[:2]

    c_all = jnp.concatenate([c_prompt, c_sample], axis=0)
    mod_all = _ada_vectors(c_all, ada_w, ada_b)
    kv_all = _ada_vectors(c_all, kv_ada_w[None], kv_ada_b[None])[0]

    def split(vecs, n_vec):
        pr = [vecs[:n_p, k * d:(k + 1) * d].reshape(n_p, 1, d) for k in range(n_vec)]
        sa = [jnp.repeat(vecs[n_p:, k * d:(k + 1) * d], t_s, axis=0) for k in range(n_vec)]
        return pr, sa

    mods_p, mods_s = zip(*[split(mod_all[l], N_MOD) for l in range(n_layers)])
    kv_p, kv_s = split(kv_all, 2)

    w = dict(
        norm_g=norm_g, kv_norm_g=kv_norm_g, norm_f=norm_f, ssm_d=ssm_d, glu_b=glu_b, sb_bias=sb_bias,
        ffn_w_gate=ffn_w_gate.astype(BF16), ffn_w_up=ffn_w_up.astype(BF16), ffn_w_down=ffn_w_down.astype(BF16),
        glu_w=glu_w.astype(BF16), w_kv=w_kv.astype(BF16), w_kv_t=w_kv.T.astype(BF16),
        w_q=w_q.astype(BF16), w_o=w_o.astype(BF16),
        s5_tables=[_s5_tables(ssm_a_re[l], ssm_a_im[l], ssm_log_dt[l], ssm_b_re[l], ssm_b_im[l],
                              ssm_c_re[l], ssm_c_im[l]) for l in range(n_a)],
    )

    zeros = jnp.zeros((n_a, n_p, state_ssm_re.shape[2] * state_ssm_re.shape[3]), F32)
    y_p, k_p, v_p, re_p, im_p = _trunk(x_prompt, mods_p, kv_p, False, zeros, zeros, None, w)

    h0_re = state_ssm_re.reshape(n_a, n_s, -1)
    h0_im = state_ssm_im.reshape(n_a, n_s, -1)
    to_cm = lambda c: c.transpose(0, 2, 3, 1).reshape(n_pool, d, page)
    past = (to_cm(cache_k), to_cm(cache_v), page_table)
    y_s, k_s, v_s, re_s, im_s = _trunk(x_sample, mods_s, kv_s, True, h0_re, h0_im, past, w)

    from_cm = lambda a: a.reshape(n_p, n_heads, hd, t_p).transpose(0, 3, 1, 2)
    st_shape_p = (n_a, n_p) + state_ssm_re.shape[2:]
    st_shape_s = (n_a, n_s) + state_ssm_re.shape[2:]
    return (y_p.reshape(n_p, t_p, d), y_s.reshape(n_s, t_s, d),
            from_cm(k_p), from_cm(v_p),
            k_s.reshape(n_s, t_s, n_heads, hd), v_s.reshape(n_s, t_s, n_heads, hd),
            jnp.stack(re_p).reshape(st_shape_p), jnp.stack(im_p).reshape(st_shape_p),
            jnp.stack(re_s).reshape(st_shape_s), jnp.stack(im_s).reshape(st_shape_s))
```

```python
import functools

import jax
import jax.numpy as jnp
from jax import lax
from jax.experimental import pallas as pl
from jax.experimental.pallas import tpu as pltpu

EPS = 1e-6
GROUP_CH = 16
N_MOD = 9
LANES = 128
MXU_TILE = 256
HEADS_PER_STAGE = 2
V7X_VMEM_LIMIT = 56 * 1024 * 1024

F32 = jnp.float32
BF16 = jnp.bfloat16


def _dot(a, b):
    return jnp.dot(a, b, preferred_element_type=F32)


def _dot_nt(a, b):
    return lax.dot_general(a, b, (((1,), (1,)), ((), ())), preferred_element_type=F32)


def _split_bf16(x):
    hi = x.astype(BF16)
    lo = (x - hi.astype(F32)).astype(BF16)
    return hi, lo


def _mod_norm(x, g, shift, scale):
    xn = x * lax.rsqrt(jnp.mean(x * x, axis=-1, keepdims=True) + EPS)
    return (xn * g) * (1.0 + scale) + shift


def _params(sem):
    return pltpu.CompilerParams(dimension_semantics=sem, vmem_limit_bytes=V7X_VMEM_LIMIT)


def _resident(shape, index_map):
    return pl.BlockSpec(shape, index_map, pipeline_mode=pl.Buffered(1))


def _ada_kernel(c_ref, w_ref, b_ref, o_ref):
    c = c_ref[...]
    sc_hi, sc_lo = _split_bf16(c * jax.nn.sigmoid(c))
    w_hi, w_lo = _split_bf16(w_ref[...])
    o_ref[...] = _dot(sc_hi, w_hi) + (_dot(sc_hi, w_lo) + _dot(sc_lo, w_hi)) + b_ref[...]


def _ada_vectors(c, w, b):
    n_l, d, n = w.shape
    m = c.shape[0]
    tn = max(t for t in range(LANES, min(n, 1024) + 1, LANES) if n % t == 0)
    return pl.pallas_call(
        _ada_kernel,
        grid=(n_l, n // tn),
        in_specs=[pl.BlockSpec((m, d), lambda l, j: (0, 0)),
                  pl.BlockSpec((None, d, tn), lambda l, j: (l, 0, j)),
                  pl.BlockSpec((None, 1, tn), lambda l, j: (l, 0, j))],
        out_specs=pl.BlockSpec((None, m, tn), lambda l, j: (l, 0, j)),
        out_shape=jax.ShapeDtypeStruct((n_l, m, n), F32),
        compiler_params=_params(("arbitrary", "arbitrary")),
    )(c, w, b.reshape(n_l, 1, n))


def _row_specs(n_rows, seq_len, d, tm, per_row):
    x_spec = pl.BlockSpec((tm, d), lambda i: (i, 0))
    if per_row:
        return x_spec, x_spec
    blocks_per_seq = seq_len // tm
    return x_spec, pl.BlockSpec((None, 1, d), lambda i: (i // blocks_per_seq, 0, 0))


def _pick_tm(seq_len, n_rows, per_row, cap):
    if per_row:
        return n_rows
    tm = min(cap, seq_len)
    assert seq_len % tm == 0
    return tm


def _ffn_kernel(*refs, n_chunks, fc, post, emit_x):
    x_ref, sh_ref, sc_ref, gt_ref, g_ref, wg_ref, wu_ref, wd_ref = refs[:8]
    pos = 8
    if post == "mod":
        pg_ref, psh_ref, psc_ref = refs[pos:pos + 3]
        pos += 3
    elif post == "plain":
        pg_ref = refs[pos]
        pos += 1
    outs = list(refs[pos:-1])
    acc_ref = refs[-1]

    x = x_ref[...]
    h = _mod_norm(x, g_ref[...], sh_ref[...], sc_ref[...]).astype(BF16)
    for c in range(n_chunks):
        lo = c * fc
        gate = _dot(h, wg_ref[:, lo:lo + fc])
        up = _dot(h, wu_ref[:, lo:lo + fc])
        a = ((gate * jax.nn.sigmoid(gate)) * up).astype(BF16)
        part = _dot(a, wd_ref[lo:lo + fc, :])
        if c == 0:
            acc_ref[...] = part
        else:
            acc_ref[...] += part
    x_new = x + (0.5 * gt_ref[...]) * acc_ref[...]
    if emit_x:
        outs.pop(0)[...] = x_new
    if post == "mod":
        outs.pop(0)[...] = _mod_norm(x_new, pg_ref[...], psh_ref[...], psc_ref[...])
    elif post == "plain":
        xn = x_new * lax.rsqrt(jnp.mean(x_new * x_new, axis=-1, keepdims=True) + EPS)
        outs.pop(0)[...] = xn * pg_ref[...]


def _ffn(x, seq_len, shift, scale, gate, g, wg, wu, wd, w_idx, *, per_row, post=None, post_args=(), emit_x=True):
    n_rows, d = x.shape
    f = wg.shape[-1]
    tm = _pick_tm(seq_len, n_rows, per_row, 512)
    fc = 256 if f % 256 == 0 else f
    x_spec, v_spec = _row_specs(n_rows, seq_len, d, tm, per_row)
    g_spec = pl.BlockSpec((1, d), lambda i: (0, 0))
    w_spec = lambda r, c: _resident((None, None, r, c), lambda i: (w_idx[0], w_idx[1], 0, 0))
    in_specs = [x_spec, v_spec, v_spec, v_spec, g_spec, w_spec(d, f), w_spec(d, f), w_spec(f, d)]
    args = [x, shift, scale, gate, g.reshape(1, d), wg, wu, wd]
    if post == "mod":
        in_specs += [g_spec, v_spec, v_spec]
        args += [post_args[0].reshape(1, d), post_args[1], post_args[2]]
    elif post == "plain":
        in_specs += [g_spec]
        args += [post_args[0].reshape(1, d)]
    out_specs, out_shape = [], []
    if emit_x:
        out_specs.append(x_spec)
        out_shape.append(jax.ShapeDtypeStruct((n_rows, d), F32))
    if post is not None:
        out_specs.append(x_spec)
        out_shape.append(jax.ShapeDtypeStruct((n_rows, d), F32))
    outs = pl.pallas_call(
        functools.partial(_ffn_kernel, n_chunks=f // fc, fc=fc, post=post, emit_x=emit_x),
        grid=(n_rows // tm,),
        in_specs=in_specs,
        out_specs=out_specs,
        out_shape=out_shape,
        scratch_shapes=[pltpu.VMEM((tm, d), F32)],
        compiler_params=_params(("arbitrary",)),
    )(*args)
    return outs


def _ssm_kernel(u_ref, bmat_ref, cmat_ref, are_ref, aim_ref, d_ref, h0re_ref, h0im_ref,
                y_ref, sre_ref, sim_ref, hs_ref, hre_ref, him_ref, *, n_b, tt, n_half, n_blk, pitch):
    n_slab = n_half // LANES
    j = pl.program_id(0)

    @pl.when(j == 0)
    def _():
        hre_ref[...] = h0re_ref[...]
        him_ref[...] = h0im_ref[...]

    def channel_block(k, _):
        c0 = pl.multiple_of(k * LANES, LANES)
        s0 = pl.multiple_of(k * n_half, n_half)
        u = jnp.concatenate([u_ref[b, :, pl.ds(c0, LANES)] for b in range(n_b)], axis=0)
        bu = _dot(u.astype(BF16), bmat_ref[k])
        for s in range(2 * n_slab):
            for b in range(n_b):
                hs_ref[s, b * pitch:b * pitch + tt, :] = bu[b * tt:(b + 1) * tt, s * LANES:(s + 1) * LANES]
        a_re = [jnp.broadcast_to(are_ref[k, :, s * LANES:(s + 1) * LANES], (n_b, LANES)) for s in range(n_slab)]
        a_im = [jnp.broadcast_to(aim_ref[k, :, s * LANES:(s + 1) * LANES], (n_b, LANES)) for s in range(n_slab)]

        def step(t, carry):
            h_re, h_im = carry
            new_re, new_im = [], []
            for s in range(n_slab):
                rows = pl.ds(t, n_b, stride=pitch)
                nr = (a_re[s] * h_re[s] - a_im[s] * h_im[s]) + hs_ref[s, rows, :]
                ni = (a_re[s] * h_im[s] + a_im[s] * h_re[s]) + hs_ref[n_slab + s, rows, :]
                hs_ref[s, rows, :] = nr
                hs_ref[n_slab + s, rows, :] = ni
                new_re.append(nr)
                new_im.append(ni)
            return tuple(new_re), tuple(new_im)

        h0 = (tuple(hre_ref[:, pl.ds(s0 + s * LANES, LANES)] for s in range(n_slab)),
              tuple(him_ref[:, pl.ds(s0 + s * LANES, LANES)] for s in range(n_slab)))
        h_re, h_im = lax.fori_loop(0, tt, step, h0, unroll=2)
        for s in range(n_slab):
            hre_ref[:, pl.ds(s0 + s * LANES, LANES)] = h_re[s]
            him_ref[:, pl.ds(s0 + s * LANES, LANES)] = h_im[s]
        hs = jnp.concatenate(
            [jnp.concatenate([hs_ref[s, b * pitch:b * pitch + tt, :] for s in range(2 * n_slab)], axis=1)
             for b in range(n_b)], axis=0)
        y = _dot(hs.astype(BF16), cmat_ref[k]) + d_ref[:, pl.ds(c0, LANES)] * u
        for b in range(n_b):
            y_ref[b, :, pl.ds(c0, LANES)] = y[b * tt:(b + 1) * tt, :]
        return 0

    lax.fori_loop(0, n_blk, channel_block, 0)

    @pl.when(j == pl.num_programs(0) - 1)
    def _():
        sre_ref[...] = hre_ref[...]
        sim_ref[...] = him_ref[...]


def _s5_tables(a_re, a_im, log_dt, b_re, b_im, c_re, c_im):
    n_g, n_s = a_re.shape
    gpb = LANES // GROUP_CH
    n_blk = n_g // gpb
    dt = jnp.exp(log_dt)[:, None]
    mag = jnp.exp(a_re * dt)
    abar_re = mag * jnp.cos(a_im * dt)
    abar_im = mag * jnp.sin(a_im * dt)
    den = a_re * a_re + a_im * a_im
    xr, xi = abar_re - 1.0, abar_im
    q_re = (xr * a_re + xi * a_im) / den
    q_im = (xi * a_re - xr * a_im) / den
    bb_re = q_re[:, :, None] * b_re - q_im[:, :, None] * b_im
    bb_im = q_re[:, :, None] * b_im + q_im[:, :, None] * b_re
    eye = jnp.eye(gpb, dtype=F32)

    def b_table(bb):
        t = bb.reshape(n_blk, gpb, n_s, GROUP_CH).transpose(0, 1, 3, 2)
        t = t[:, :, :, None, :] * eye[None, :, None, :, None]
        return t.reshape(n_blk, gpb * GROUP_CH, gpb * n_s)

    def c_table(cc):
        t = cc.reshape(n_blk, gpb, GROUP_CH, n_s).transpose(0, 1, 3, 2)
        t = t[:, :, :, None, :] * eye[None, :, None, :, None]
        return t.reshape(n_blk, gpb * n_s, gpb * GROUP_CH)

    bmat = jnp.concatenate([b_table(bb_re), b_table(bb_im)], axis=2).astype(BF16)
    cmat = jnp.concatenate([c_table(c_re), c_table(-c_im)], axis=1).astype(BF16)
    n_half = gpb * n_s
    return (bmat, cmat, abar_re.reshape(n_blk, 1, n_half), abar_im.reshape(n_blk, 1, n_half))


def _s5_scan(u, n_b, tables, d_skip, h0_re, h0_im):
    bmat, cmat, abar_re, abar_im = tables
    n_rows, d = u.shape
    t_len = n_rows // n_b
    n_blk, _, n_half = abar_re.shape
    n_state = n_blk * n_half
    tt = min(64, t_len)
    assert t_len % tt == 0
    nt = t_len // tt
    pitch = tt + 8
    seq_spec = pl.BlockSpec((n_b, tt, d), lambda j: (0, j, 0))
    full = lambda shape: _resident(shape, lambda j: (0,) * len(shape))
    y, s_re, s_im = pl.pallas_call(
        functools.partial(_ssm_kernel, n_b=n_b, tt=tt, n_half=n_half, n_blk=n_blk, pitch=pitch),
        grid=(nt,),
        in_specs=[seq_spec, full(bmat.shape), full(cmat.shape), full(abar_re.shape), full(abar_im.shape),
                  full((1, d)), full((n_b, n_state)), full((n_b, n_state))],
        out_specs=[seq_spec] + [pl.BlockSpec((n_b, n_state), lambda j: (0, 0))] * 2,
        out_shape=[jax.ShapeDtypeStruct((n_b, t_len, d), F32)] + [jax.ShapeDtypeStruct((n_b, n_state), F32)] * 2,
        scratch_shapes=[pltpu.VMEM((2 * n_half // LANES, n_b * pitch, LANES), F32),
                        pltpu.VMEM((n_b, n_state), F32), pltpu.VMEM((n_b, n_state), F32)],
        compiler_params=_params(("arbitrary",)),
    )(u.reshape(n_b, t_len, d), bmat, cmat, abar_re, abar_im, d_skip.reshape(1, d), h0_re, h0_im)
    return y.reshape(n_rows, d), s_re, s_im


def _glu_kernel(y_ref, x_ref, gt_ref, w_ref, b_ref, o_ref):
    d = x_ref.shape[-1]
    ge = jax.nn.gelu(y_ref[...]).astype(BF16)
    gl = _dot(ge, w_ref[...]) + b_ref[...]
    mix = gl[:, :d] * jax.nn.sigmoid(gl[:, d:])
    o_ref[...] = x_ref[...] + gt_ref[...] * mix


def _glu_residual(y, x, seq_len, gate, w, b, *, per_row):
    n_rows, d = x.shape
    tm = _pick_tm(seq_len, n_rows, per_row, 512)
    x_spec, v_spec = _row_specs(n_rows, seq_len, d, tm, per_row)
    return pl.pallas_call(
        _glu_kernel,
        grid=(n_rows // tm,),
        in_specs=[x_spec, x_spec, v_spec, _resident((d, 2 * d), lambda i: (0, 0)),
                  pl.BlockSpec((1, 2 * d), lambda i: (0, 0))],
        out_specs=x_spec,
        out_shape=jax.ShapeDtypeStruct((n_rows, d), F32),
        compiler_params=_params(("arbitrary",)),
    )(y, x, gate, w, b.reshape(1, 2 * d))


def _proj_kernel(x_ref, sh_ref, sc_ref, g_ref, w_ref, *o_refs, out_scale):
    d = x_ref.shape[-1]
    h = _mod_norm(x_ref[...], g_ref[...], sh_ref[...], sc_ref[...]).astype(BF16)
    for k, o_ref in enumerate(o_refs):
        o_ref[...] = (_dot(h, w_ref[:, k * d:(k + 1) * d]) * out_scale).astype(o_ref.dtype)


def _norm_proj(x, seq_len, shift, scale, g, w, *, per_row, out_dtype, out_scale=1.0):
    n_rows, d = x.shape
    n_out = w.shape[1] // d
    tm = _pick_tm(seq_len, n_rows, per_row, 512)
    x_spec, v_spec = _row_specs(n_rows, seq_len, d, tm, per_row)
    return pl.pallas_call(
        functools.partial(_proj_kernel, out_scale=out_scale),
        grid=(n_rows // tm,),
        in_specs=[x_spec, v_spec, v_spec, pl.BlockSpec((1, d), lambda i: (0, 0)),
                  _resident((d, n_out * d), lambda i: (0, 0))],
        out_specs=[x_spec] * n_out,
        out_shape=[jax.ShapeDtypeStruct((n_rows, d), out_dtype)] * n_out,
        compiler_params=_params(("arbitrary",)),
    )(x, shift, scale, g.reshape(1, d), w)


def _proj_t_kernel(x_ref, sh_ref, sc_ref, g_ref, wt_ref, *o_refs):
    d = x_ref.shape[-1]
    h = _mod_norm(x_ref[...], g_ref[...], sh_ref[...], sc_ref[...]).astype(BF16)
    for k, o_ref in enumerate(o_refs):
        o_ref[...] = _dot_nt(wt_ref[k * d:(k + 1) * d, :], h)


def _norm_proj_t(x, n_seq, seq_len, shift, scale, g, wt):
    n_rows, d = x.shape
    n_out = wt.shape[0] // d
    tm = _pick_tm(seq_len, n_rows, False, 512)
    nt = seq_len // tm
    x_spec, v_spec = _row_specs(n_rows, seq_len, d, tm, False)
    o_spec = pl.BlockSpec((None, d, tm), lambda i: (i // nt, 0, i % nt))
    return pl.pallas_call(
        _proj_t_kernel,
        grid=(n_rows // tm,),
        in_specs=[x_spec, v_spec, v_spec, pl.BlockSpec((1, d), lambda i: (0, 0)),
                  _resident((n_out * d, d), lambda i: (0, 0))],
        out_specs=[o_spec] * n_out,
        out_shape=[jax.ShapeDtypeStruct((n_seq, d, seq_len), F32)] * n_out,
        compiler_params=_params(("arbitrary",)),
    )(x, shift, scale, g.reshape(1, d), wt)


def _oproj_kernel(o_ref, x_ref, gt_ref, w_ref, out_ref):
    out_ref[...] = x_ref[...] + gt_ref[...] * _dot(o_ref[...], w_ref[...])


def _oproj_residual(o, x, seq_len, gate, w, *, per_row):
    n_rows, d = x.shape
    tm = _pick_tm(seq_len, n_rows, per_row, 512)
    x_spec, v_spec = _row_specs(n_rows, seq_len, d, tm, per_row)
    return pl.pallas_call(
        _oproj_kernel,
        grid=(n_rows // tm,),
        in_specs=[x_spec, x_spec, v_spec, _resident((d, d), lambda i: (0, 0))],
        out_specs=x_spec,
        out_shape=jax.ShapeDtypeStruct((n_rows, d), F32),
        compiler_params=_params(("arbitrary",)),
    )(o, x, gate, w)


def _log_beta_pair(z):
    sp = jnp.log(1.0 + jnp.exp(-jnp.abs(z)))
    return jnp.minimum(z, 0.0) - sp, -jnp.maximum(z, 0.0) - sp


def _suffix_sum_lanes(l1m, tri):
    hi, lo = _split_bf16(l1m)
    return _dot(hi, tri) + _dot(lo, tri)


def _prompt_attn_kernel(bias_ref, q_ref, kt_ref, vt_ref, o_ref, ktb_ref, vtm_ref, *, tq, hd):
    grp = pl.program_id(1)
    i = pl.program_id(2)
    n_hh = MXU_TILE // hd

    @pl.when(i == 0)
    def _():
        ktb_ref[...] = kt_ref[...].astype(BF16)
        vt = vt_ref[...]
        ch_head = lax.broadcasted_iota(jnp.int32, vt.shape, 0) // hd
        for hh in range(n_hh):
            vtm_ref[hh] = jnp.where(ch_head == hh, vt, 0.0).astype(BF16)

    q2 = q_ref[...]
    lane_head = lax.broadcasted_iota(jnp.int32, (tq, MXU_TILE), 1) // hd
    r_idx = lax.broadcasted_iota(jnp.int32, (tq, tq), 0)
    c_idx = lax.broadcasted_iota(jnp.int32, (tq, tq), 1)
    tri = (r_idx > c_idx).astype(BF16)
    valid = c_idx < r_idx
    qns = [jnp.where(lane_head == hh, -q2, jnp.zeros_like(q2)) for hh in range(n_hh)]
    biases = [bias_ref[grp * n_hh + hh] for hh in range(n_hh)]
    heads = range(n_hh)

    def block(j, carry, masked):
        c_sums, acc = carry
        c0 = pl.multiple_of(j * tq, tq)
        kt = ktb_ref[:, pl.ds(c0, tq)]
        groups = [range(h0, min(h0 + HEADS_PER_STAGE, n_hh)) for h0 in range(0, n_hh, HEADS_PER_STAGE)]
        zns, l1bs, log_bs, firsts, sufs, ws, new_sums = {}, {}, {}, {}, {}, {}, {}

        def scores(hs):
            for hh in hs:
                zns[hh] = _dot(qns[hh], kt) - biases[hh]

        def log_terms(hs):
            for hh in hs:
                neg_abs = pltpu.bitcast(pltpu.bitcast(zns[hh], jnp.uint32) | jnp.uint32(0x80000000), F32)
                l1m = jnp.minimum(zns[hh], 0.0) - jnp.log(1.0 + jnp.exp(neg_abs))
                if masked:
                    l1m = jnp.where(valid, l1m, 0.0)
                log_bs[hh] = l1m - zns[hh]
                firsts[hh] = l1m[:, 0:1]
                l1bs[hh] = l1m.astype(BF16)

        def suffix_sums(hs):
            for hh in hs:
                sufs[hh] = _dot(l1bs[hh], tri)

        def weights(hs):
            for hh in hs:
                w = jnp.exp(log_bs[hh] + sufs[hh] + c_sums[hh])
                ws[hh] = (jnp.where(valid, w, 0.0) if masked else w).astype(BF16)
                new_sums[hh] = c_sums[hh] + (sufs[hh][:, 0:1] + firsts[hh])

        for hs in groups:
            scores(hs)
            log_terms(hs)
            suffix_sums(hs)
            weights(hs)
            for hh in hs:
                acc = acc + _dot_nt(ws[hh], vtm_ref[hh, :, pl.ds(c0, tq)])
        return tuple(new_sums[hh] for hh in heads), acc

    carry = (tuple(jnp.zeros((tq, 1), F32) for _ in range(n_hh)), jnp.zeros((tq, MXU_TILE), F32))
    carry = block(i, carry, True)
    carry = lax.fori_loop(0, i, lambda jj, cr: block(i - 1 - jj, cr, False), carry)
    o_ref[...] = carry[1].astype(o_ref.dtype)


def _prompt_attention(q, kt, vt, bias, n_seq, seq_len, hd):
    n_rows, d = q.shape
    tq = min(256, seq_len)
    nq = seq_len // tq
    n_grp = d // MXU_TILE
    kv_spec = pl.BlockSpec((None, MXU_TILE, seq_len), lambda b, p, i: (b, p, 0))
    return pl.pallas_call(
        functools.partial(_prompt_attn_kernel, tq=tq, hd=hd),
        grid=(n_seq, n_grp, nq),
        in_specs=[pl.BlockSpec(memory_space=pltpu.SMEM),
                  pl.BlockSpec((tq, MXU_TILE), lambda b, p, i: (b * nq + i, p)),
                  kv_spec, kv_spec],
        out_specs=pl.BlockSpec((tq, MXU_TILE), lambda b, p, i: (b * nq + i, p)),
        out_shape=jax.ShapeDtypeStruct((n_rows, d), BF16),
        scratch_shapes=[pltpu.VMEM((MXU_TILE, seq_len), BF16),
                        pltpu.VMEM((MXU_TILE // hd, MXU_TILE, seq_len), BF16)],
        compiler_params=_params(("arbitrary", "arbitrary", "arbitrary")),
    )(bias, q, kt, vt)


def _sample_attn_kernel(pt_ref, qb_ref, bias_ref, knt_ref, vnt_ref, *rest, n_pp, page, n_q, hd):
    k_refs = rest[:n_pp]
    v_refs = rest[n_pp:2 * n_pp]
    o_ref, acc_ref, c_ref = rest[2 * n_pp:]
    j = pl.program_id(1)
    n_tile, rows_t, _ = qb_ref.shape
    n_col = n_tile * rows_t
    bias = bias_ref[...]
    r_idx = lax.broadcasted_iota(jnp.int32, (page, page), 0)
    c_idx = lax.broadcasted_iota(jnp.int32, (page, page), 1)
    tri = (r_idx > c_idx).astype(BF16)

    def scores(kt_ref, valid):
        zs = [_dot(qb_ref[t], kt_ref[t * MXU_TILE:(t + 1) * MXU_TILE, :].astype(BF16)) for t in range(n_tile)]
        z = jnp.concatenate(zs, axis=0) + bias
        log_b, l1m = _log_beta_pair(z)
        if valid is not None:
            l1m = jnp.where(valid, l1m, 0.0)
        suf = _suffix_sum_lanes(l1m, tri)
        return log_b, suf, suf[:, 0:1] + l1m[:, 0:1]

    def weighted_values(log_b, suf, c_sum, vt_ref, valid):
        w = jnp.exp(log_b + suf + c_sum)
        if valid is not None:
            w = jnp.where(valid, w, 0.0)
        wb = w.astype(BF16)
        outs = [_dot_nt(wb[t * rows_t:(t + 1) * rows_t, :], vt_ref[t * MXU_TILE:(t + 1) * MXU_TILE, :].astype(BF16))
                for t in range(n_tile)]
        return jnp.concatenate(outs, axis=0)

    @pl.when(j == 0)
    def _():
        key_i = lax.broadcasted_iota(jnp.int32, (n_col, page), 1)
        qry_i = lax.broadcasted_iota(jnp.int32, (n_col, page), 0) % n_q
        valid = key_i < qry_i
        log_b, suf, tot = scores(knt_ref, valid)
        acc_ref[...] = weighted_values(log_b, suf, jnp.zeros((n_col, 1), F32), vnt_ref, valid)
        c_ref[...] = tot

    @pl.when(j > 0)
    def _():
        parts = [scores(k_refs[pp], None) for pp in range(n_pp)]
        c_sum = c_ref[...]
        acc = acc_ref[...]
        for pp in range(n_pp):
            log_b, suf, tot = parts[pp]
            acc = acc + weighted_values(log_b, suf, c_sum, v_refs[pp], None)
            c_sum = c_sum + tot
        acc_ref[...] = acc
        c_ref[...] = c_sum

    @pl.when(j == pl.num_programs(1) - 1)
    def _():
        row_h = lax.broadcasted_iota(jnp.int32, (rows_t, MXU_TILE), 0) // n_q
        col_h = lax.broadcasted_iota(jnp.int32, (rows_t, MXU_TILE), 1) // hd
        outs = []
        for t in range(n_tile):
            picked = jnp.where(row_h == col_h, acc_ref[t * rows_t:(t + 1) * rows_t, :], 0.0)
            out_t = picked[0:n_q, :]
            for g in range(1, rows_t // n_q):
                out_t = out_t + picked[g * n_q:(g + 1) * n_q, :]
            outs.append(out_t)
        o_ref[...] = jnp.concatenate(outs, axis=1).astype(o_ref.dtype)


def _sample_attention(q, k_new, v_new, cache_kt, cache_vt, page_table, bias, n_q, hd):
    n_rows, d = q.shape
    n_seq = n_rows // n_q
    n_pool, _, page = cache_kt.shape
    n_pages = page_table.shape[1]
    n_pp = max(p for p in (16, 8, 4, 2, 1) if n_pages % p == 0)
    n_steps = n_pages // n_pp
    n_tile = d // MXU_TILE
    hpt = MXU_TILE // hd
    rows_t = hpt * n_q
    n_col = n_tile * rows_t
    q5 = q.reshape(n_seq, n_q, n_tile, hpt, hd).transpose(0, 2, 3, 1, 4)
    qb = q5[:, :, :, :, None, :] * jnp.eye(hpt, dtype=q.dtype)[None, None, :, None, :, None]
    qb = qb.reshape(n_seq, n_tile, rows_t, MXU_TILE)
    bias_rows = jnp.repeat(bias.astype(F32), n_q).reshape(n_col, 1)

    def new_page(a):
        a = jnp.pad(a.reshape(n_seq, n_q, d), ((0, 0), (0, page - n_q), (0, 0)))
        return a.transpose(0, 2, 1)

    seq3 = lambda r, c: pl.BlockSpec((None, r, c), lambda b, j, pt: (b, 0, 0))

    def page_spec(pp):
        def index_map(b, j, pt):
            group = jnp.maximum(j, 1) - 1
            return (pt[b, n_pages - 1 - group * n_pp - pp], 0, 0)
        return pl.BlockSpec((None, d, page), index_map)

    grid_spec = pltpu.PrefetchScalarGridSpec(
        num_scalar_prefetch=1,
        grid=(n_seq, n_steps + 1),
        in_specs=[pl.BlockSpec((None, n_tile, rows_t, MXU_TILE), lambda b, j, pt: (b, 0, 0, 0)),
                  pl.BlockSpec((n_col, 1), lambda b, j, pt: (0, 0)),
                  seq3(d, page), seq3(d, page)]
                 + [page_spec(pp) for pp in range(n_pp)] * 2,
        out_specs=seq3(n_q, d),
        scratch_shapes=[pltpu.VMEM((n_col, MXU_TILE), F32), pltpu.VMEM((n_col, 1), F32)],
    )
    out = pl.pallas_call(
        functools.partial(_sample_attn_kernel, n_pp=n_pp, page=page, n_q=n_q, hd=hd),
        grid_spec=grid_spec,
        out_shape=jax.ShapeDtypeStruct((n_seq, n_q, d), BF16),
        compiler_params=_params(("arbitrary", "arbitrary")),
    )(page_table, qb, bias_rows, new_page(k_new), new_page(v_new), *([cache_kt] * n_pp), *([cache_vt] * n_pp))
    return out.reshape(n_rows, d)


def _trunk(x3, mods, kv_mod, per_row, h0_re, h0_im, past, w):
    n_seq, t_len, d = x3.shape
    n_layers = len(mods)
    n_a = n_layers // 2
    hd = d // w["sb_bias"].shape[1]
    x = x3.reshape(n_seq * t_len, d)
    new_re, new_im = [], []
    k_new = v_new = None
    for l in range(n_layers):
        m = mods[l]
        g = w["norm_g"][l]
        if l == n_a:
            if past is None:
                k_new, v_new = _norm_proj_t(x, n_seq, t_len, kv_mod[0], kv_mod[1], w["kv_norm_g"], w["w_kv_t"])
            else:
                k_new, v_new = _norm_proj(x, t_len, kv_mod[0], kv_mod[1], w["kv_norm_g"], w["w_kv"],
                                          per_row=per_row, out_dtype=F32)
        ffn1 = (w["ffn_w_gate"], w["ffn_w_up"], w["ffn_w_down"], (l, 0))
        ffn2 = (w["ffn_w_gate"], w["ffn_w_up"], w["ffn_w_down"], (l, 1))
        if l < n_a:
            x, u = _ffn(x, t_len, m[0], m[1], m[2], g[0], *ffn1, per_row=per_row, post="mod",
                        post_args=(g[1], m[3], m[4]))
            y, s_re, s_im = _s5_scan(u, n_seq, w["s5_tables"][l], w["ssm_d"][l], h0_re[l], h0_im[l])
            x = _glu_residual(y, x, t_len, m[5], w["glu_w"][l], w["glu_b"][l], per_row=per_row)
            new_re.append(s_re)
            new_im.append(s_im)
        else:
            jb = l - n_a
            (x,) = _ffn(x, t_len, m[0], m[1], m[2], g[0], *ffn1, per_row=per_row)
            (q,) = _norm_proj(x, t_len, m[3], m[4], g[1], w["w_q"][jb], per_row=per_row, out_dtype=BF16,
                              out_scale=hd ** -0.5)
            if past is None:
                o = _prompt_attention(q, k_new, v_new, w["sb_bias"][jb], n_seq, t_len, hd)
            else:
                o = _sample_attention(q, k_new, v_new, past[0], past[1], past[2], w["sb_bias"][jb], t_len, hd)
            x = _oproj_residual(o, x, t_len, m[5], w["w_o"][jb], per_row=per_row)
        if l == n_layers - 1:
            (x,) = _ffn(x, t_len, m[6], m[7], m[8], g[2], *ffn2, per_row=per_row, post="plain",
                        post_args=(w["norm_f"],), emit_x=False)
        else:
            (x,) = _ffn(x, t_len, m[6], m[7], m[8], g[2], *ffn2, per_row=per_row)
    return x, k_new, v_new, new_re, new_im


def kernel(x_prompt, x_sample, cache_k, cache_v, state_ssm_re, state_ssm_im, page_table,
           c_prompt, c_sample, ada_w, ada_b, norm_g, ffn_w_gate, ffn_w_up, ffn_w_down,
           ssm_a_re, ssm_a_im, ssm_log_dt, ssm_b_re, ssm_b_im, ssm_c_re, ssm_c_im, ssm_d,
           glu_w, glu_b, kv_ada_w, kv_ada_b, kv_norm_g, w_kv, w_q, w_o, sb_bias, norm_f):
    n_p, t_p, d = x_prompt.shape
    n_s, t_s, _ = x_sample.shape
    n_layers = ada_w.shape[0]
    n_a = n_layers // 2
    n_heads = sb_bias.shape[1]
    hd = d // n_heads
    n_pool, page = cache_k.shape[:2]

    c_all = jnp.concatenate([c_prompt, c_sample], axis=0)
    mod_all = _ada_vectors(c_all, ada_w, ada_b)
    kv_all = _ada_vectors(c_all, kv_ada_w[None], kv_ada_b[None])[0]

    def split(vecs, n_vec):
        pr = [vecs[:n_p, k * d:(k + 1) * d].reshape(n_p, 1, d) for k in range(n_vec)]
        sa = [jnp.repeat(vecs[n_p:, k * d:(k + 1) * d], t_s, axis=0) for k in range(n_vec)]
        return pr, sa

    mods_p, mods_s = zip(*[split(mod_all[l], N_MOD) for l in range(n_layers)])
    kv_p, kv_s = split(kv_all, 2)

    w = dict(
        norm_g=norm_g, kv_norm_g=kv_norm_g, norm_f=norm_f, ssm_d=ssm_d, glu_b=glu_b, sb_bias=sb_bias,
        ffn_w_gate=ffn_w_gate.astype(BF16), ffn_w_up=ffn_w_up.astype(BF16), ffn_w_down=ffn_w_down.astype(BF16),
        glu_w=glu_w.astype(BF16), w_kv=w_kv.astype(BF16), w_kv_t=w_kv.T.astype(BF16),
        w_q=w_q.astype(BF16), w_o=w_o.astype(BF16),
        s5_tables=[_s5_tables(ssm_a_re[l], ssm_a_im[l], ssm_log_dt[l], ssm_b_re[l], ssm_b_im[l],
                              ssm_c_re[l], ssm_c_im[l]) for l in range(n_a)],
    )

    zeros = jnp.zeros((n_a, n_p, state_ssm_re.shape[2] * state_ssm_re.shape[3]), F32)
    y_p, k_p, v_p, re_p, im_p = _trunk(x_prompt, mods_p, kv_p, False, zeros, zeros, None, w)

    h0_re = state_ssm_re.reshape(n_a, n_s, -1)
    h0_im = state_ssm_im.reshape(n_a, n_s, -1)
    to_cm = lambda c: c.transpose(0, 2, 3, 1).reshape(n_pool, d, page)
    past = (to_cm(cache_k), to_cm(cache_v), page_table)
    y_s, k_s, v_s, re_s, im_s = _trunk(x_sample, mods_s, kv_s, True, h0_re, h0_im, past, w)

    from_cm = lambda a: a.reshape(n_p, n_heads, hd, t_p).transpose(0, 3, 1, 2)
    st_shape_p = (n_a, n_p) + state_ssm_re.shape[2:]
    st_shape_s = (n_a, n_s) + state_ssm_re.shape[2:]
    return (y_p.reshape(n_p, t_p, d), y_s.reshape(n_s, t_s, d),
            from_cm(k_p), from_cm(v_p),
            k_s.reshape(n_s, t_s, n_heads, hd), v_s.reshape(n_s, t_s, n_heads, hd),
            jnp.stack(re_p).reshape(st_shape_p), jnp.stack(im_p).reshape(st_shape_p),
            jnp.stack(re_s).reshape(st_shape_s), jnp.stack(im_s).reshape(st_shape_s))
```

```python
import functools

import jax
import jax.numpy as jnp
from jax import lax
from jax.experimental import pallas as pl
from jax.experimental.pallas import tpu as pltpu

EPS = 1e-6
GROUP_CH = 16
N_MOD = 9
LANES = 128
MXU_TILE = 256
HEADS_PER_STAGE = 2
V7X_VMEM_LIMIT = 56 * 1024 * 1024

F32 = jnp.float32
BF16 = jnp.bfloat16


def _dot(a, b):
    return jnp.dot(a, b, preferred_element_type=F32)


def _dot_nt(a, b):
    return lax.dot_general(a, b, (((1,), (1,)), ((), ())), preferred_element_type=F32)


def _split_bf16(x):
    hi = x.astype(BF16)
    lo = (x - hi.astype(F32)).astype(BF16)
    return hi, lo


def _mod_norm(x, g, shift, scale):
    xn = x * lax.rsqrt(jnp.mean(x * x, axis=-1, keepdims=True) + EPS)
    return (xn * g) * (1.0 + scale) + shift


def _params(sem):
    return pltpu.CompilerParams(dimension_semantics=sem, vmem_limit_bytes=V7X_VMEM_LIMIT)


def _resident(shape, index_map):
    return pl.BlockSpec(shape, index_map, pipeline_mode=pl.Buffered(1))


def _ada_kernel(c_ref, w_ref, b_ref, o_ref):
    c = c_ref[...]
    sc_hi, sc_lo = _split_bf16(c * jax.nn.sigmoid(c))
    w_hi, w_lo = _split_bf16(w_ref[...])
    o_ref[...] = _dot(sc_hi, w_hi) + (_dot(sc_hi, w_lo) + _dot(sc_lo, w_hi)) + b_ref[...]


def _ada_vectors(c, w, b):
    n_l, d, n = w.shape
    m = c.shape[0]
    tn = max(t for t in range(LANES, min(n, 1024) + 1, LANES) if n % t == 0)
    return pl.pallas_call(
        _ada_kernel,
        grid=(n_l, n // tn),
        in_specs=[pl.BlockSpec((m, d), lambda l, j: (0, 0)),
                  pl.BlockSpec((None, d, tn), lambda l, j: (l, 0, j)),
                  pl.BlockSpec((None, 1, tn), lambda l, j: (l, 0, j))],
        out_specs=pl.BlockSpec((None, m, tn), lambda l, j: (l, 0, j)),
        out_shape=jax.ShapeDtypeStruct((n_l, m, n), F32),
        compiler_params=_params(("arbitrary", "arbitrary")),
    )(c, w, b.reshape(n_l, 1, n))


def _row_specs(n_rows, seq_len, d, tm, per_row):
    x_spec = pl.BlockSpec((tm, d), lambda i: (i, 0))
    if per_row:
        return x_spec, x_spec
    blocks_per_seq = seq_len // tm
    return x_spec, pl.BlockSpec((None, 1, d), lambda i: (i // blocks_per_seq, 0, 0))


def _pick_tm(seq_len, n_rows, per_row, cap):
    if per_row:
        return n_rows
    tm = min(cap, seq_len)
    assert seq_len % tm == 0
    return tm


def _ffn_kernel(*refs, n_chunks, fc, pre, post, emit_x, proj_scale):
    x_ref, sh_ref, sc_ref, gt_ref, g_ref, wg_ref, wu_ref, wd_ref = refs[:8]
    pos = 8
    n_pre = {None: 0, "glu": 4, "oproj": 3}[pre]
    n_post = {None: 0, "mod": 3, "plain": 1, "proj": 4}[post]
    pre_refs = refs[pos:pos + n_pre]
    post_refs = refs[pos + n_pre:pos + n_pre + n_post]
    outs = list(refs[pos + n_pre + n_post:-1])
    acc_ref = refs[-1]

    x = x_ref[...]
    d = x.shape[-1]
    if pre == "glu":
        y_ref, pgt_ref, pw_ref, pb_ref = pre_refs
        gl = _dot(jax.nn.gelu(y_ref[...]).astype(BF16), pw_ref[...]) + pb_ref[...]
        x = x + pgt_ref[...] * (gl[:, :d] * jax.nn.sigmoid(gl[:, d:]))
    elif pre == "oproj":
        o_ref, pgt_ref, pw_ref = pre_refs
        x = x + pgt_ref[...] * _dot(o_ref[...], pw_ref[...])
    h = _mod_norm(x, g_ref[...], sh_ref[...], sc_ref[...]).astype(BF16)
    for c in range(n_chunks):
        lo = c * fc
        gate = _dot(h, wg_ref[:, lo:lo + fc])
        up = _dot(h, wu_ref[:, lo:lo + fc])
        a = ((gate * jax.nn.sigmoid(gate)) * up).astype(BF16)
        part = _dot(a, wd_ref[lo:lo + fc, :])
        if c == 0:
            acc_ref[...] = part
        else:
            acc_ref[...] += part
    x_new = x + (0.5 * gt_ref[...]) * acc_ref[...]
    if emit_x:
        outs.pop(0)[...] = x_new
    if post in ("mod", "proj"):
        hp = _mod_norm(x_new, post_refs[0][...], post_refs[1][...], post_refs[2][...])
        if post == "mod":
            outs.pop(0)[...] = hp
        else:
            p_ref = outs.pop(0)
            p_ref[...] = (_dot(hp.astype(BF16), post_refs[3][...]) * proj_scale).astype(p_ref.dtype)
    elif post == "plain":
        xn = x_new * lax.rsqrt(jnp.mean(x_new * x_new, axis=-1, keepdims=True) + EPS)
        outs.pop(0)[...] = xn * post_refs[0][...]


def _ffn(x, seq_len, shift, scale, gate, g, wg, wu, wd, w_idx, *, per_row, pre=None, pre_args=(),
         post=None, post_args=(), emit_x=True, proj_scale=1.0):
    n_rows, d = x.shape
    f = wg.shape[-1]
    tm = _pick_tm(seq_len, n_rows, per_row, 512)
    fc = 256 if f % 256 == 0 else f
    x_spec, v_spec = _row_specs(n_rows, seq_len, d, tm, per_row)
    g_spec = pl.BlockSpec((1, d), lambda i: (0, 0))
    w_spec = lambda r, c: _resident((None, None, r, c), lambda i: (w_idx[0], w_idx[1], 0, 0))
    in_specs = [x_spec, v_spec, v_spec, v_spec, g_spec, w_spec(d, f), w_spec(d, f), w_spec(f, d)]
    args = [x, shift, scale, gate, g.reshape(1, d), wg, wu, wd]
    if pre == "glu":
        in_specs += [x_spec, v_spec, _resident((d, 2 * d), lambda i: (0, 0)),
                     pl.BlockSpec((1, 2 * d), lambda i: (0, 0))]
        args += [pre_args[0], pre_args[1], pre_args[2], pre_args[3].reshape(1, 2 * d)]
    elif pre == "oproj":
        in_specs += [x_spec, v_spec, _resident((d, d), lambda i: (0, 0))]
        args += list(pre_args)
    if post in ("mod", "proj"):
        in_specs += [g_spec, v_spec, v_spec]
        args += [post_args[0].reshape(1, d), post_args[1], post_args[2]]
        if post == "proj":
            in_specs += [_resident((d, d), lambda i: (0, 0))]
            args += [post_args[3]]
    elif post == "plain":
        in_specs += [g_spec]
        args += [post_args[0].reshape(1, d)]
    out_specs, out_shape = [], []
    if emit_x:
        out_specs.append(x_spec)
        out_shape.append(jax.ShapeDtypeStruct((n_rows, d), F32))
    if post is not None:
        out_specs.append(x_spec)
        out_shape.append(jax.ShapeDtypeStruct((n_rows, d), BF16 if post == "proj" else F32))
    outs = pl.pallas_call(
        functools.partial(_ffn_kernel, n_chunks=f // fc, fc=fc, pre=pre, post=post, emit_x=emit_x,
                          proj_scale=proj_scale),
        grid=(n_rows // tm,),
        in_specs=in_specs,
        out_specs=out_specs,
        out_shape=out_shape,
        scratch_shapes=[pltpu.VMEM((tm, d), F32)],
        compiler_params=_params(("arbitrary",)),
    )(*args)
    return outs


def _ssm_kernel(u_ref, bmat_ref, cmat_ref, are_ref, aim_ref, d_ref, h0re_ref, h0im_ref,
                y_ref, sre_ref, sim_ref, hs_ref, hre_ref, him_ref, *, n_b, tt, n_half, n_blk, pitch):
    n_slab = n_half // LANES
    j = pl.program_id(0)

    @pl.when(j == 0)
    def _():
        hre_ref[...] = h0re_ref[...]
        him_ref[...] = h0im_ref[...]

    def channel_block(k, _):
        c0 = pl.multiple_of(k * LANES, LANES)
        s0 = pl.multiple_of(k * n_half, n_half)
        u = jnp.concatenate([u_ref[b, :, pl.ds(c0, LANES)] for b in range(n_b)], axis=0)
        bu = _dot(u.astype(BF16), bmat_ref[k])
        for s in range(2 * n_slab):
            for b in range(n_b):
                hs_ref[s, b * pitch:b * pitch + tt, :] = bu[b * tt:(b + 1) * tt, s * LANES:(s + 1) * LANES]
        a_re = [jnp.broadcast_to(are_ref[k, :, s * LANES:(s + 1) * LANES], (n_b, LANES)) for s in range(n_slab)]
        a_im = [jnp.broadcast_to(aim_ref[k, :, s * LANES:(s + 1) * LANES], (n_b, LANES)) for s in range(n_slab)]

        def step(t, carry):
            h_re, h_im = carry
            new_re, new_im = [], []
            for s in range(n_slab):
                rows = pl.ds(t, n_b, stride=pitch)
                nr = (a_re[s] * h_re[s] - a_im[s] * h_im[s]) + hs_ref[s, rows, :]
                ni = (a_re[s] * h_im[s] + a_im[s] * h_re[s]) + hs_ref[n_slab + s, rows, :]
                hs_ref[s, rows, :] = nr
                hs_ref[n_slab + s, rows, :] = ni
                new_re.append(nr)
                new_im.append(ni)
            return tuple(new_re), tuple(new_im)

        h0 = (tuple(hre_ref[:, pl.ds(s0 + s * LANES, LANES)] for s in range(n_slab)),
              tuple(him_ref[:, pl.ds(s0 + s * LANES, LANES)] for s in range(n_slab)))
        h_re, h_im = lax.fori_loop(0, tt, step, h0, unroll=2)
        for s in range(n_slab):
            hre_ref[:, pl.ds(s0 + s * LANES, LANES)] = h_re[s]
            him_ref[:, pl.ds(s0 + s * LANES, LANES)] = h_im[s]
        hs = jnp.concatenate(
            [jnp.concatenate([hs_ref[s, b * pitch:b * pitch + tt, :] for s in range(2 * n_slab)], axis=1)
             for b in range(n_b)], axis=0)
        y = _dot(hs.astype(BF16), cmat_ref[k]) + d_ref[:, pl.ds(c0, LANES)] * u
        for b in range(n_b):
            y_ref[b, :, pl.ds(c0, LANES)] = y[b * tt:(b + 1) * tt, :]
        return 0

    lax.fori_loop(0, n_blk, channel_block, 0)

    @pl.when(j == pl.num_programs(0) - 1)
    def _():
        sre_ref[...] = hre_ref[...]
        sim_ref[...] = him_ref[...]


def _s5_tables(a_re, a_im, log_dt, b_re, b_im, c_re, c_im):
    n_g, n_s = a_re.shape
    gpb = LANES // GROUP_CH
    n_blk = n_g // gpb
    dt = jnp.exp(log_dt)[:, None]
    mag = jnp.exp(a_re * dt)
    abar_re = mag * jnp.cos(a_im * dt)
    abar_im = mag * jnp.sin(a_im * dt)
    den = a_re * a_re + a_im * a_im
    xr, xi = abar_re - 1.0, abar_im
    q_re = (xr * a_re + xi * a_im) / den
    q_im = (xi * a_re - xr * a_im) / den
    bb_re = q_re[:, :, None] * b_re - q_im[:, :, None] * b_im
    bb_im = q_re[:, :, None] * b_im + q_im[:, :, None] * b_re
    eye = jnp.eye(gpb, dtype=F32)

    def b_table(bb):
        t = bb.reshape(n_blk, gpb, n_s, GROUP_CH).transpose(0, 1, 3, 2)
        t = t[:, :, :, None, :] * eye[None, :, None, :, None]
        return t.reshape(n_blk, gpb * GROUP_CH, gpb * n_s)

    def c_table(cc):
        t = cc.reshape(n_blk, gpb, GROUP_CH, n_s).transpose(0, 1, 3, 2)
        t = t[:, :, :, None, :] * eye[None, :, None, :, None]
        return t.reshape(n_blk, gpb * n_s, gpb * GROUP_CH)

    bmat = jnp.concatenate([b_table(bb_re), b_table(bb_im)], axis=2).astype(BF16)
    cmat = jnp.concatenate([c_table(c_re), c_table(-c_im)], axis=1).astype(BF16)
    n_half = gpb * n_s
    return (bmat, cmat, abar_re.reshape(n_blk, 1, n_half), abar_im.reshape(n_blk, 1, n_half))


def _s5_scan(u, n_b, tables, d_skip, h0_re, h0_im):
    bmat, cmat, abar_re, abar_im = tables
    n_rows, d = u.shape
    t_len = n_rows // n_b
    n_blk, _, n_half = abar_re.shape
    n_state = n_blk * n_half
    tt = min(64, t_len)
    assert t_len % tt == 0
    nt = t_len // tt
    pitch = tt + 8
    seq_spec = pl.BlockSpec((n_b, tt, d), lambda j: (0, j, 0))
    full = lambda shape: _resident(shape, lambda j: (0,) * len(shape))
    y, s_re, s_im = pl.pallas_call(
        functools.partial(_ssm_kernel, n_b=n_b, tt=tt, n_half=n_half, n_blk=n_blk, pitch=pitch),
        grid=(nt,),
        in_specs=[seq_spec, full(bmat.shape), full(cmat.shape), full(abar_re.shape), full(abar_im.shape),
                  full((1, d)), full((n_b, n_state)), full((n_b, n_state))],
        out_specs=[seq_spec] + [pl.BlockSpec((n_b, n_state), lambda j: (0, 0))] * 2,
        out_shape=[jax.ShapeDtypeStruct((n_b, t_len, d), F32)] + [jax.ShapeDtypeStruct((n_b, n_state), F32)] * 2,
        scratch_shapes=[pltpu.VMEM((2 * n_half // LANES, n_b * pitch, LANES), F32),
                        pltpu.VMEM((n_b, n_state), F32), pltpu.VMEM((n_b, n_state), F32)],
        compiler_params=_params(("arbitrary",)),
    )(u.reshape(n_b, t_len, d), bmat, cmat, abar_re, abar_im, d_skip.reshape(1, d), h0_re, h0_im)
    return y.reshape(n_rows, d), s_re, s_im


def _proj_kernel(x_ref, sh_ref, sc_ref, g_ref, w_ref, *o_refs, out_scale):
    d = x_ref.shape[-1]
    h = _mod_norm(x_ref[...], g_ref[...], sh_ref[...], sc_ref[...]).astype(BF16)
    for k, o_ref in enumerate(o_refs):
        o_ref[...] = (_dot(h, w_ref[:, k * d:(k + 1) * d]) * out_scale).astype(o_ref.dtype)


def _norm_proj(x, seq_len, shift, scale, g, w, *, per_row, out_dtype, out_scale=1.0):
    n_rows, d = x.shape
    n_out = w.shape[1] // d
    tm = _pick_tm(seq_len, n_rows, per_row, 512)
    x_spec, v_spec = _row_specs(n_rows, seq_len, d, tm, per_row)
    return pl.pallas_call(
        functools.partial(_proj_kernel, out_scale=out_scale),
        grid=(n_rows // tm,),
        in_specs=[x_spec, v_spec, v_spec, pl.BlockSpec((1, d), lambda i: (0, 0)),
                  _resident((d, n_out * d), lambda i: (0, 0))],
        out_specs=[x_spec] * n_out,
        out_shape=[jax.ShapeDtypeStruct((n_rows, d), out_dtype)] * n_out,
        compiler_params=_params(("arbitrary",)),
    )(x, shift, scale, g.reshape(1, d), w)


def _proj_t_kernel(x_ref, sh_ref, sc_ref, g_ref, wt_ref, *o_refs):
    d = x_ref.shape[-1]
    h = _mod_norm(x_ref[...], g_ref[...], sh_ref[...], sc_ref[...]).astype(BF16)
    for k, o_ref in enumerate(o_refs):
        o_ref[...] = _dot_nt(wt_ref[k * d:(k + 1) * d, :], h)


def _norm_proj_t(x, n_seq, seq_len, shift, scale, g, wt):
    n_rows, d = x.shape
    n_out = wt.shape[0] // d
    tm = _pick_tm(seq_len, n_rows, False, 512)
    nt = seq_len // tm
    x_spec, v_spec = _row_specs(n_rows, seq_len, d, tm, False)
    o_spec = pl.BlockSpec((None, d, tm), lambda i: (i // nt, 0, i % nt))
    return pl.pallas_call(
        _proj_t_kernel,
        grid=(n_rows // tm,),
        in_specs=[x_spec, v_spec, v_spec, pl.BlockSpec((1, d), lambda i: (0, 0)),
                  _resident((n_out * d, d), lambda i: (0, 0))],
        out_specs=[o_spec] * n_out,
        out_shape=[jax.ShapeDtypeStruct((n_seq, d, seq_len), F32)] * n_out,
        compiler_params=_params(("arbitrary",)),
    )(x, shift, scale, g.reshape(1, d), wt)


def _log_beta_pair(z):
    sp = jnp.log(1.0 + jnp.exp(-jnp.abs(z)))
    return jnp.minimum(z, 0.0) - sp, -jnp.maximum(z, 0.0) - sp


def _suffix_sum_lanes(l1m, tri):
    hi, lo = _split_bf16(l1m)
    return _dot(hi, tri) + _dot(lo, tri)


def _prompt_attn_kernel(bias_ref, q_ref, kt_ref, vt_ref, o_ref, ktb_ref, vtm_ref, *, tq, hd):
    grp = pl.program_id(1)
    i = pl.program_id(2)
    n_hh = MXU_TILE // hd

    @pl.when(i == 0)
    def _():
        ktb_ref[...] = kt_ref[...].astype(BF16)
        vt = vt_ref[...]
        ch_head = lax.broadcasted_iota(jnp.int32, vt.shape, 0) // hd
        for hh in range(n_hh):
            vtm_ref[hh] = jnp.where(ch_head == hh, vt, 0.0).astype(BF16)

    q2 = q_ref[...]
    lane_head = lax.broadcasted_iota(jnp.int32, (tq, MXU_TILE), 1) // hd
    r_idx = lax.broadcasted_iota(jnp.int32, (tq, tq), 0)
    c_idx = lax.broadcasted_iota(jnp.int32, (tq, tq), 1)
    tri = (r_idx > c_idx).astype(BF16)
    valid = c_idx < r_idx
    qns = [jnp.where(lane_head == hh, -q2, jnp.zeros_like(q2)) for hh in range(n_hh)]
    biases = [bias_ref[grp * n_hh + hh] for hh in range(n_hh)]
    heads = range(n_hh)

    def block(j, carry, masked):
        c_sums, acc = carry
        c0 = pl.multiple_of(j * tq, tq)
        kt = ktb_ref[:, pl.ds(c0, tq)]
        groups = [range(h0, min(h0 + HEADS_PER_STAGE, n_hh)) for h0 in range(0, n_hh, HEADS_PER_STAGE)]
        zns, l1bs, log_bs, firsts, sufs, ws, new_sums = {}, {}, {}, {}, {}, {}, {}

        def scores(hs):
            for hh in hs:
                zns[hh] = _dot(qns[hh], kt) - biases[hh]

        def log_terms(hs):
            for hh in hs:
                neg_abs = pltpu.bitcast(pltpu.bitcast(zns[hh], jnp.uint32) | jnp.uint32(0x80000000), F32)
                l1m = jnp.minimum(zns[hh], 0.0) - jnp.log(1.0 + jnp.exp(neg_abs))
                if masked:
                    l1m = jnp.where(valid, l1m, 0.0)
                log_bs[hh] = l1m - zns[hh]
                firsts[hh] = l1m[:, 0:1]
                l1bs[hh] = l1m.astype(BF16)

        def suffix_sums(hs):
            for hh in hs:
                sufs[hh] = _dot(l1bs[hh], tri)

        def weights(hs):
            for hh in hs:
                w = jnp.exp(log_bs[hh] + sufs[hh] + c_sums[hh])
                ws[hh] = (jnp.where(valid, w, 0.0) if masked else w).astype(BF16)
                new_sums[hh] = c_sums[hh] + (sufs[hh][:, 0:1] + firsts[hh])

        for hs in groups:
            scores(hs)
            log_terms(hs)
            suffix_sums(hs)
            weights(hs)
            for hh in hs:
                acc = acc + _dot_nt(ws[hh], vtm_ref[hh, :, pl.ds(c0, tq)])
        return tuple(new_sums[hh] for hh in heads), acc

    carry = (tuple(jnp.zeros((tq, 1), F32) for _ in range(n_hh)), jnp.zeros((tq, MXU_TILE), F32))
    carry = block(i, carry, True)
    carry = lax.fori_loop(0, i, lambda jj, cr: block(i - 1 - jj, cr, False), carry)
    o_ref[...] = carry[1].astype(o_ref.dtype)


def _prompt_attention(q, kt, vt, bias, n_seq, seq_len, hd):
    n_rows, d = q.shape
    tq = min(256, seq_len)
    nq = seq_len // tq
    n_grp = d // MXU_TILE
    kv_spec = pl.BlockSpec((None, MXU_TILE, seq_len), lambda b, p, i: (b, p, 0))
    return pl.pallas_call(
        functools.partial(_prompt_attn_kernel, tq=tq, hd=hd),
        grid=(n_seq, n_grp, nq),
        in_specs=[pl.BlockSpec(memory_space=pltpu.SMEM),
                  pl.BlockSpec((tq, MXU_TILE), lambda b, p, i: (b * nq + i, p)),
                  kv_spec, kv_spec],
        out_specs=pl.BlockSpec((tq, MXU_TILE), lambda b, p, i: (b * nq + i, p)),
        out_shape=jax.ShapeDtypeStruct((n_rows, d), BF16),
        scratch_shapes=[pltpu.VMEM((MXU_TILE, seq_len), BF16),
                        pltpu.VMEM((MXU_TILE // hd, MXU_TILE, seq_len), BF16)],
        compiler_params=_params(("arbitrary", "arbitrary", "arbitrary")),
    )(bias, q, kt, vt)


def _sample_attn_kernel(pt_ref, qb_ref, bias_ref, knt_ref, vnt_ref, *rest, n_pp, page, n_q, hd):
    k_refs = rest[:n_pp]
    v_refs = rest[n_pp:2 * n_pp]
    o_ref, acc_ref, c_ref = rest[2 * n_pp:]
    j = pl.program_id(1)
    n_tile, rows_t, _ = qb_ref.shape
    n_col = n_tile * rows_t
    bias = bias_ref[...]
    r_idx = lax.broadcasted_iota(jnp.int32, (page, page), 0)
    c_idx = lax.broadcasted_iota(jnp.int32, (page, page), 1)
    tri = (r_idx > c_idx).astype(BF16)

    def scores(kt_ref, valid):
        zs = [_dot(qb_ref[t], kt_ref[t * MXU_TILE:(t + 1) * MXU_TILE, :].astype(BF16)) for t in range(n_tile)]
        z = jnp.concatenate(zs, axis=0) + bias
        log_b, l1m = _log_beta_pair(z)
        if valid is not None:
            l1m = jnp.where(valid, l1m, 0.0)
        suf = _suffix_sum_lanes(l1m, tri)
        return log_b, suf, suf[:, 0:1] + l1m[:, 0:1]

    def weighted_values(log_b, suf, c_sum, vt_ref, valid):
        w = jnp.exp(log_b + suf + c_sum)
        if valid is not None:
            w = jnp.where(valid, w, 0.0)
        wb = w.astype(BF16)
        outs = [_dot_nt(wb[t * rows_t:(t + 1) * rows_t, :], vt_ref[t * MXU_TILE:(t + 1) * MXU_TILE, :].astype(BF16))
                for t in range(n_tile)]
        return jnp.concatenate(outs, axis=0)

    @pl.when(j == 0)
    def _():
        key_i = lax.broadcasted_iota(jnp.int32, (n_col, page), 1)
        qry_i = lax.broadcasted_iota(jnp.int32, (n_col, page), 0) % n_q
        valid = key_i < qry_i
        log_b, suf, tot = scores(knt_ref, valid)
        acc_ref[...] = weighted_values(log_b, suf, jnp.zeros((n_col, 1), F32), vnt_ref, valid)
        c_ref[...] = tot

    @pl.when(j > 0)
    def _():
        parts = [scores(k_refs[pp], None) for pp in range(n_pp)]
        c_sum = c_ref[...]
        acc = acc_ref[...]
        for pp in range(n_pp):
            log_b, suf, tot = parts[pp]
            acc = acc + weighted_values(log_b, suf, c_sum, v_refs[pp], None)
            c_sum = c_sum + tot
        acc_ref[...] = acc
        c_ref[...] = c_sum

    @pl.when(j == pl.num_programs(1) - 1)
    def _():
        row_h = lax.broadcasted_iota(jnp.int32, (rows_t, MXU_TILE), 0) // n_q
        col_h = lax.broadcasted_iota(jnp.int32, (rows_t, MXU_TILE), 1) // hd
        outs = []
        for t in range(n_tile):
            picked = jnp.where(row_h == col_h, acc_ref[t * rows_t:(t + 1) * rows_t, :], 0.0)
            out_t = picked[0:n_q, :]
            for g in range(1, rows_t // n_q):
                out_t = out_t + picked[g * n_q:(g + 1) * n_q, :]
            outs.append(out_t)
        o_ref[...] = jnp.concatenate(outs, axis=1).astype(o_ref.dtype)


def _sample_attention(q, k_new, v_new, cache_kt, cache_vt, page_table, bias, n_q, hd):
    n_rows, d = q.shape
    n_seq = n_rows // n_q
    n_pool, _, page = cache_kt.shape
    n_pages = page_table.shape[1]
    n_pp = max(p for p in (16, 8, 4, 2, 1) if n_pages % p == 0)
    n_steps = n_pages // n_pp
    n_tile = d // MXU_TILE
    hpt = MXU_TILE // hd
    rows_t = hpt * n_q
    n_col = n_tile * rows_t
    q5 = q.reshape(n_seq, n_q, n_tile, hpt, hd).transpose(0, 2, 3, 1, 4)
    qb = q5[:, :, :, :, None, :] * jnp.eye(hpt, dtype=q.dtype)[None, None, :, None, :, None]
    qb = qb.reshape(n_seq, n_tile, rows_t, MXU_TILE)
    bias_rows = jnp.repeat(bias.astype(F32), n_q).reshape(n_col, 1)

    def new_page(a):
        a = jnp.pad(a.reshape(n_seq, n_q, d), ((0, 0), (0, page - n_q), (0, 0)))
        return a.transpose(0, 2, 1)

    seq3 = lambda r, c: pl.BlockSpec((None, r, c), lambda b, j, pt: (b, 0, 0))

    def page_spec(pp):
        def index_map(b, j, pt):
            group = jnp.maximum(j, 1) - 1
            return (pt[b, n_pages - 1 - group * n_pp - pp], 0, 0)
        return pl.BlockSpec((None, d, page), index_map)

    grid_spec = pltpu.PrefetchScalarGridSpec(
        num_scalar_prefetch=1,
        grid=(n_seq, n_steps + 1),
        in_specs=[pl.BlockSpec((None, n_tile, rows_t, MXU_TILE), lambda b, j, pt: (b, 0, 0, 0)),
                  pl.BlockSpec((n_col, 1), lambda b, j, pt: (0, 0)),
                  seq3(d, page), seq3(d, page)]
                 + [page_spec(pp) for pp in range(n_pp)] * 2,
        out_specs=seq3(n_q, d),
        scratch_shapes=[pltpu.VMEM((n_col, MXU_TILE), F32), pltpu.VMEM((n_col, 1), F32)],
    )
    out = pl.pallas_call(
        functools.partial(_sample_attn_kernel, n_pp=n_pp, page=page, n_q=n_q, hd=hd),
        grid_spec=grid_spec,
        out_shape=jax.ShapeDtypeStruct((n_seq, n_q, d), BF16),
        compiler_params=_params(("arbitrary", "arbitrary")),
    )(page_table, qb, bias_rows, new_page(k_new), new_page(v_new), *([cache_kt] * n_pp), *([cache_vt] * n_pp))
    return out.reshape(n_rows, d)


def _trunk(x3, mods, kv_mod, per_row, h0_re, h0_im, past, w):
    n_seq, t_len, d = x3.shape
    n_layers = len(mods)
    n_a = n_layers // 2
    hd = d // w["sb_bias"].shape[1]
    x = x3.reshape(n_seq * t_len, d)
    new_re, new_im = [], []
    k_new = v_new = None
    for l in range(n_layers):
        m = mods[l]
        g = w["norm_g"][l]
        if l == n_a:
            if past is None:
                k_new, v_new = _norm_proj_t(x, n_seq, t_len, kv_mod[0], kv_mod[1], w["kv_norm_g"], w["w_kv_t"])
            else:
                k_new, v_new = _norm_proj(x, t_len, kv_mod[0], kv_mod[1], w["kv_norm_g"], w["w_kv"],
                                          per_row=per_row, out_dtype=F32)
        ffn1 = (w["ffn_w_gate"], w["ffn_w_up"], w["ffn_w_down"], (l, 0))
        ffn2 = (w["ffn_w_gate"], w["ffn_w_up"], w["ffn_w_down"], (l, 1))
        if l < n_a:
            x, u = _ffn(x, t_len, m[0], m[1], m[2], g[0], *ffn1, per_row=per_row, post="mod",
                        post_args=(g[1], m[3], m[4]))
            y, s_re, s_im = _s5_scan(u, n_seq, w["s5_tables"][l], w["ssm_d"][l], h0_re[l], h0_im[l])
            mixer = dict(pre="glu", pre_args=(y, m[5], w["glu_w"][l], w["glu_b"][l]))
            new_re.append(s_re)
            new_im.append(s_im)
        else:
            jb = l - n_a
            x, q = _ffn(x, t_len, m[0], m[1], m[2], g[0], *ffn1, per_row=per_row, post="proj",
                        post_args=(g[1], m[3], m[4], w["w_q"][jb]), proj_scale=hd ** -0.5)
            if past is None:
                o = _prompt_attention(q, k_new, v_new, w["sb_bias"][jb], n_seq, t_len, hd)
            else:
                o = _sample_attention(q, k_new, v_new, past[0], past[1], past[2], w["sb_bias"][jb], t_len, hd)
            mixer = dict(pre="oproj", pre_args=(o, m[5], w["w_o"][jb]))
        if l == n_layers - 1:
            (x,) = _ffn(x, t_len, m[6], m[7], m[8], g[2], *ffn2, per_row=per_row, post="plain",
                        post_args=(w["norm_f"],), emit_x=False, **mixer)
        else:
            (x,) = _ffn(x, t_len, m[6], m[7], m[8], g[2], *ffn2, per_row=per_row, **mixer)
    return x, k_new, v_new, new_re, new_im


def kernel(x_prompt, x_sample, cache_k, cache_v, state_ssm_re, state_ssm_im, page_table,
           c_prompt, c_sample, ada_w, ada_b, norm_g, ffn_w_gate, ffn_w_up, ffn_w_down,
           ssm_a_re, ssm_a_im, ssm_log_dt, ssm_b_re, ssm_b_im, ssm_c_re, ssm_c_im, ssm_d,
           glu_w, glu_b, kv_ada_w, kv_ada_b, kv_norm_g, w_kv, w_q, w_o, sb_bias, norm_f):
    n_p, t_p, d = x_prompt.shape
    n_s, t_s, _ = x_sample.shape
    n_layers = ada_w.shape[0]
    n_a = n_layers // 2
    n_heads = sb_bias.shape[1]
    hd = d // n_heads
    n_pool, page = cache_k.shape[:2]

    c_all = jnp.concatenate([c_prompt, c_sample], axis=0)
    mod_all = _ada_vectors(c_all, ada_w, ada_b)
    kv_all = _ada_vectors(c_all, kv_ada_w[None], kv_ada_b[None])[0]

    def split(vecs, n_vec):
        pr = [vecs[:n_p, k * d:(k + 1) * d].reshape(n_p, 1, d) for k in range(n_vec)]
        sa = [jnp.repeat(vecs[n_p:, k * d:(k + 1) * d], t_s, axis=0) for k in range(n_vec)]
        return pr, sa

    mods_p, mods_s = zip(*[split(mod_all[l], N_MOD) for l in range(n_layers)])
    kv_p, kv_s = split(kv_all, 2)

    w = dict(
        norm_g=norm_g, kv_norm_g=kv_norm_g, norm_f=norm_f, ssm_d=ssm_d, glu_b=glu_b, sb_bias=sb_bias,
        ffn_w_gate=ffn_w_gate.astype(BF16), ffn_w_up=ffn_w_up.astype(BF16), ffn_w_down=ffn_w_down.astype(BF16),
        glu_w=glu_w.astype(BF16), w_kv=w_kv.astype(BF16), w_kv_t=w_kv.T.astype(BF16),
        w_q=w_q.astype(BF16), w_o=w_o.astype(BF16),
        s5_tables=[_s5_tables(ssm_a_re[l], ssm_a_im[l], ssm_log_dt[l], ssm_b_re[l], ssm_b_im[l],
                              ssm_c_re[l], ssm_c_im[l]) for l in range(n_a)],
    )

    zeros = jnp.zeros((n_a, n_p, state_ssm_re.shape[2] * state_ssm_re.shape[3]), F32)
    y_p, k_p, v_p, re_p, im_p = _trunk(x_prompt, mods_p, kv_p, False, zeros, zeros, None, w)

    h0_re = state_ssm_re.reshape(n_a, n_s, -1)
    h0_im = state_ssm_im.reshape(n_a, n_s, -1)
    to_cm = lambda c: c.transpose(0, 2, 3, 1).reshape(n_pool, d, page)
    past = (to_cm(cache_k), to_cm(cache_v), page_table)
    y_s, k_s, v_s, re_s, im_s = _trunk(x_sample, mods_s, kv_s, True, h0_re, h0_im, past, w)

    from_cm = lambda a: a.reshape(n_p, n_heads, hd, t_p).transpose(0, 3, 1, 2)
    st_shape_p = (n_a, n_p) + state_ssm_re.shape[2:]
    st_shape_s = (n_a, n_s) + state_ssm_re.shape[2:]
    return (y_p.reshape(n_p, t_p, d), y_s.reshape(n_s, t_s, d),
            from_cm(k_p), from_cm(v_p),
            k_s.reshape(n_s, t_s, n_heads, hd), v_s.reshape(n_s, t_s, n_heads, hd),
            jnp.stack(re_p).reshape(st_shape_p), jnp.stack(im_p).reshape(st_shape_p),
            jnp.stack(re_s).reshape(st_shape_s), jnp.stack(im_s).reshape(st_shape_s))
```

```python
import functools

import jax
import jax.numpy as jnp
from jax import lax
from jax.experimental import pallas as pl
from jax.experimental.pallas import tpu as pltpu

EPS = 1e-6
GROUP_CH = 16
N_MOD = 9
LANES = 128
MXU_TILE = 256
HEADS_PER_STAGE = 2
V7X_VMEM_LIMIT = 56 * 1024 * 1024

F32 = jnp.float32
BF16 = jnp.bfloat16


def _dot(a, b):
    return jnp.dot(a, b, preferred_element_type=F32)


def _dot_nt(a, b):
    return lax.dot_general(a, b, (((1,), (1,)), ((), ())), preferred_element_type=F32)


def _split_bf16(x):
    hi = x.astype(BF16)
    lo = (x - hi.astype(F32)).astype(BF16)
    return hi, lo


def _mod_norm(x, g, shift, scale):
    xn = x * lax.rsqrt(jnp.mean(x * x, axis=-1, keepdims=True) + EPS)
    return (xn * g) * (1.0 + scale) + shift


def _params(sem):
    return pltpu.CompilerParams(dimension_semantics=sem, vmem_limit_bytes=V7X_VMEM_LIMIT)


def _resident(shape, index_map):
    return pl.BlockSpec(shape, index_map, pipeline_mode=pl.Buffered(1))


def _ada_kernel(c_ref, w_ref, b_ref, o_ref):
    c = c_ref[...]
    sc_hi, sc_lo = _split_bf16(c * jax.nn.sigmoid(c))
    w_hi, w_lo = _split_bf16(w_ref[...])
    o_ref[...] = _dot(sc_hi, w_hi) + (_dot(sc_hi, w_lo) + _dot(sc_lo, w_hi)) + b_ref[...]


def _ada_vectors(c, w, b):
    n_l, d, n = w.shape
    m = c.shape[0]
    tn = max(t for t in range(LANES, min(n, 1024) + 1, LANES) if n % t == 0)
    return pl.pallas_call(
        _ada_kernel,
        grid=(n_l, n // tn),
        in_specs=[pl.BlockSpec((m, d), lambda l, j: (0, 0)),
                  pl.BlockSpec((None, d, tn), lambda l, j: (l, 0, j)),
                  pl.BlockSpec((None, 1, tn), lambda l, j: (l, 0, j))],
        out_specs=pl.BlockSpec((None, m, tn), lambda l, j: (l, 0, j)),
        out_shape=jax.ShapeDtypeStruct((n_l, m, n), F32),
        compiler_params=_params(("arbitrary", "arbitrary")),
    )(c, w, b.reshape(n_l, 1, n))


def _row_specs(n_rows, seq_len, d, tm, per_row):
    x_spec = pl.BlockSpec((tm, d), lambda i: (i, 0))
    if per_row:
        return x_spec, x_spec
    blocks_per_seq = seq_len // tm
    return x_spec, pl.BlockSpec((None, 1, d), lambda i: (i // blocks_per_seq, 0, 0))


def _pick_tm(seq_len, n_rows, per_row, cap):
    if per_row:
        return n_rows
    tm = min(cap, seq_len)
    assert seq_len % tm == 0
    return tm


def _ffn_kernel(*refs, n_chunks, fc, pre, post, emit_x, proj_scale):
    x_ref, sh_ref, sc_ref, gt_ref, g_ref, wg_ref, wu_ref, wd_ref = refs[:8]
    pos = 8
    n_pre = {None: 0, "glu": 4, "oproj": 3}[pre]
    n_post = {None: 0, "mod": 3, "plain": 1, "proj": 4}[post]
    pre_refs = refs[pos:pos + n_pre]
    post_refs = refs[pos + n_pre:pos + n_pre + n_post]
    outs = list(refs[pos + n_pre + n_post:-1])
    acc_ref = refs[-1]

    x = x_ref[...]
    d = x.shape[-1]
    if pre == "glu":
        y_ref, pgt_ref, pw_ref, pb_ref = pre_refs
        gl = _dot(jax.nn.gelu(y_ref[...]).astype(BF16), pw_ref[...]) + pb_ref[...]
        x = x + pgt_ref[...] * (gl[:, :d] * jax.nn.sigmoid(gl[:, d:]))
    elif pre == "oproj":
        o_ref, pgt_ref, pw_ref = pre_refs
        x = x + pgt_ref[...] * _dot(o_ref[...], pw_ref[...])
    h = _mod_norm(x, g_ref[...], sh_ref[...], sc_ref[...]).astype(BF16)
    for c in range(n_chunks):
        lo = c * fc
        gate = _dot(h, wg_ref[:, lo:lo + fc])
        up = _dot(h, wu_ref[:, lo:lo + fc])
        a = ((gate * jax.nn.sigmoid(gate)) * up).astype(BF16)
        part = _dot(a, wd_ref[lo:lo + fc, :])
        if c == 0:
            acc_ref[...] = part
        else:
            acc_ref[...] += part
    x_new = x + (0.5 * gt_ref[...]) * acc_ref[...]
    if emit_x:
        outs.pop(0)[...] = x_new
    if post in ("mod", "proj"):
        hp = _mod_norm(x_new, post_refs[0][...], post_refs[1][...], post_refs[2][...])
        if post == "mod":
            outs.pop(0)[...] = hp
        else:
            p_ref = outs.pop(0)
            p_ref[...] = (_dot(hp.astype(BF16), post_refs[3][...]) * proj_scale).astype(p_ref.dtype)
    elif post == "plain":
        xn = x_new * lax.rsqrt(jnp.mean(x_new * x_new, axis=-1, keepdims=True) + EPS)
        outs.pop(0)[...] = xn * post_refs[0][...]


def _ffn(x, seq_len, shift, scale, gate, g, wg, wu, wd, w_idx, *, per_row, pre=None, pre_args=(),
         post=None, post_args=(), emit_x=True, proj_scale=1.0):
    n_rows, d = x.shape
    f = wg.shape[-1]
    tm = _pick_tm(seq_len, n_rows, per_row, 512)
    fc = 256 if f % 256 == 0 else f
    x_spec, v_spec = _row_specs(n_rows, seq_len, d, tm, per_row)
    g_spec = pl.BlockSpec((1, d), lambda i: (0, 0))
    w_spec = lambda r, c: _resident((None, None, r, c), lambda i: (w_idx[0], w_idx[1], 0, 0))
    in_specs = [x_spec, v_spec, v_spec, v_spec, g_spec, w_spec(d, f), w_spec(d, f), w_spec(f, d)]
    args = [x, shift, scale, gate, g.reshape(1, d), wg, wu, wd]
    if pre == "glu":
        in_specs += [x_spec, v_spec, _resident((d, 2 * d), lambda i: (0, 0)),
                     pl.BlockSpec((1, 2 * d), lambda i: (0, 0))]
        args += [pre_args[0], pre_args[1], pre_args[2], pre_args[3].reshape(1, 2 * d)]
    elif pre == "oproj":
        in_specs += [x_spec, v_spec, _resident((d, d), lambda i: (0, 0))]
        args += list(pre_args)
    if post in ("mod", "proj"):
        in_specs += [g_spec, v_spec, v_spec]
        args += [post_args[0].reshape(1, d), post_args[1], post_args[2]]
        if post == "proj":
            in_specs += [_resident((d, d), lambda i: (0, 0))]
            args += [post_args[3]]
    elif post == "plain":
        in_specs += [g_spec]
        args += [post_args[0].reshape(1, d)]
    out_specs, out_shape = [], []
    if emit_x:
        out_specs.append(x_spec)
        out_shape.append(jax.ShapeDtypeStruct((n_rows, d), F32))
    if post is not None:
        out_specs.append(x_spec)
        out_shape.append(jax.ShapeDtypeStruct((n_rows, d), BF16 if post == "proj" else F32))
    outs = pl.pallas_call(
        functools.partial(_ffn_kernel, n_chunks=f // fc, fc=fc, pre=pre, post=post, emit_x=emit_x,
                          proj_scale=proj_scale),
        grid=(n_rows // tm,),
        in_specs=in_specs,
        out_specs=out_specs,
        out_shape=out_shape,
        scratch_shapes=[pltpu.VMEM((tm, d), F32)],
        compiler_params=_params(("arbitrary",)),
    )(*args)
    return outs


def _ssm_kernel(u_ref, bmat_ref, cmat_ref, are_ref, aim_ref, d_ref, h0re_ref, h0im_ref,
                y_ref, sre_ref, sim_ref, hs_ref, hre_ref, him_ref, *, n_b, tt, n_half, n_blk, pitch):
    n_slab = n_half // LANES
    j = pl.program_id(0)

    @pl.when(j == 0)
    def _():
        hre_ref[...] = h0re_ref[...]
        him_ref[...] = h0im_ref[...]

    def channel_block(k, _):
        c0 = pl.multiple_of(k * LANES, LANES)
        s0 = pl.multiple_of(k * n_half, n_half)
        u = jnp.concatenate([u_ref[b, :, pl.ds(c0, LANES)] for b in range(n_b)], axis=0)
        bu = _dot(u.astype(BF16), bmat_ref[k])
        for s in range(2 * n_slab):
            for b in range(n_b):
                hs_ref[s, b * pitch:b * pitch + tt, :] = bu[b * tt:(b + 1) * tt, s * LANES:(s + 1) * LANES]
        a_re = [jnp.broadcast_to(are_ref[k, :, s * LANES:(s + 1) * LANES], (n_b, LANES)) for s in range(n_slab)]
        a_im = [jnp.broadcast_to(aim_ref[k, :, s * LANES:(s + 1) * LANES], (n_b, LANES)) for s in range(n_slab)]

        def step(t, carry):
            h_re, h_im = carry
            new_re, new_im = [], []
            for s in range(n_slab):
                rows = pl.ds(t, n_b, stride=pitch)
                nr = (a_re[s] * h_re[s] - a_im[s] * h_im[s]) + hs_ref[s, rows, :]
                ni = (a_re[s] * h_im[s] + a_im[s] * h_re[s]) + hs_ref[n_slab + s, rows, :]
                hs_ref[s, rows, :] = nr
                hs_ref[n_slab + s, rows, :] = ni
                new_re.append(nr)
                new_im.append(ni)
            return tuple(new_re), tuple(new_im)

        h0 = (tuple(hre_ref[:, pl.ds(s0 + s * LANES, LANES)] for s in range(n_slab)),
              tuple(him_ref[:, pl.ds(s0 + s * LANES, LANES)] for s in range(n_slab)))
        h_re, h_im = lax.fori_loop(0, tt, step, h0, unroll=2)
        for s in range(n_slab):
            hre_ref[:, pl.ds(s0 + s * LANES, LANES)] = h_re[s]
            him_ref[:, pl.ds(s0 + s * LANES, LANES)] = h_im[s]
        hs = jnp.concatenate(
            [jnp.concatenate([hs_ref[s, b * pitch:b * pitch + tt, :] for s in range(2 * n_slab)], axis=1)
             for b in range(n_b)], axis=0)
        y = _dot(hs.astype(BF16), cmat_ref[k]) + d_ref[:, pl.ds(c0, LANES)] * u
        for b in range(n_b):
            y_ref[b, :, pl.ds(c0, LANES)] = y[b * tt:(b + 1) * tt, :]
        return 0

    lax.fori_loop(0, n_blk, channel_block, 0)

    @pl.when(j == pl.num_programs(0) - 1)
    def _():
        sre_ref[...] = hre_ref[...]
        sim_ref[...] = him_ref[...]


def _s5_tables(a_re, a_im, log_dt, b_re, b_im, c_re, c_im):
    n_g, n_s = a_re.shape
    gpb = LANES // GROUP_CH
    n_blk = n_g // gpb
    dt = jnp.exp(log_dt)[:, None]
    mag = jnp.exp(a_re * dt)
    abar_re = mag * jnp.cos(a_im * dt)
    abar_im = mag * jnp.sin(a_im * dt)
    den = a_re * a_re + a_im * a_im
    xr, xi = abar_re - 1.0, abar_im
    q_re = (xr * a_re + xi * a_im) / den
    q_im = (xi * a_re - xr * a_im) / den
    bb_re = q_re[:, :, None] * b_re - q_im[:, :, None] * b_im
    bb_im = q_re[:, :, None] * b_im + q_im[:, :, None] * b_re
    eye = jnp.eye(gpb, dtype=F32)

    def b_table(bb):
        t = bb.reshape(n_blk, gpb, n_s, GROUP_CH).transpose(0, 1, 3, 2)
        t = t[:, :, :, None, :] * eye[None, :, None, :, None]
        return t.reshape(n_blk, gpb * GROUP_CH, gpb * n_s)

    def c_table(cc):
        t = cc.reshape(n_blk, gpb, GROUP_CH, n_s).transpose(0, 1, 3, 2)
        t = t[:, :, :, None, :] * eye[None, :, None, :, None]
        return t.reshape(n_blk, gpb * n_s, gpb * GROUP_CH)

    bmat = jnp.concatenate([b_table(bb_re), b_table(bb_im)], axis=2).astype(BF16)
    cmat = jnp.concatenate([c_table(c_re), c_table(-c_im)], axis=1).astype(BF16)
    n_half = gpb * n_s
    return (bmat, cmat, abar_re.reshape(n_blk, 1, n_half), abar_im.reshape(n_blk, 1, n_half))


def _s5_scan(u, n_b, tables, d_skip, h0_re, h0_im):
    bmat, cmat, abar_re, abar_im = tables
    n_rows, d = u.shape
    t_len = n_rows // n_b
    n_blk, _, n_half = abar_re.shape
    n_state = n_blk * n_half
    tt = min(64, t_len)
    assert t_len % tt == 0
    nt = t_len // tt
    pitch = tt + 8
    seq_spec = pl.BlockSpec((n_b, tt, d), lambda j: (0, j, 0))
    full = lambda shape: _resident(shape, lambda j: (0,) * len(shape))
    y, s_re, s_im = pl.pallas_call(
        functools.partial(_ssm_kernel, n_b=n_b, tt=tt, n_half=n_half, n_blk=n_blk, pitch=pitch),
        grid=(nt,),
        in_specs=[seq_spec, full(bmat.shape), full(cmat.shape), full(abar_re.shape), full(abar_im.shape),
                  full((1, d)), full((n_b, n_state)), full((n_b, n_state))],
        out_specs=[seq_spec] + [pl.BlockSpec((n_b, n_state), lambda j: (0, 0))] * 2,
        out_shape=[jax.ShapeDtypeStruct((n_b, t_len, d), F32)] + [jax.ShapeDtypeStruct((n_b, n_state), F32)] * 2,
        scratch_shapes=[pltpu.VMEM((2 * n_half // LANES, n_b * pitch, LANES), F32),
                        pltpu.VMEM((n_b, n_state), F32), pltpu.VMEM((n_b, n_state), F32)],
        compiler_params=_params(("arbitrary",)),
    )(u.reshape(n_b, t_len, d), bmat, cmat, abar_re, abar_im, d_skip.reshape(1, d), h0_re, h0_im)
    return y.reshape(n_rows, d), s_re, s_im


def _proj_kernel(x_ref, sh_ref, sc_ref, g_ref, w_ref, *o_refs, out_scale):
    d = x_ref.shape[-1]
    h = _mod_norm(x_ref[...], g_ref[...], sh_ref[...], sc_ref[...]).astype(BF16)
    for k, o_ref in enumerate(o_refs):
        o_ref[...] = (_dot(h, w_ref[:, k * d:(k + 1) * d]) * out_scale).astype(o_ref.dtype)


def _norm_proj(x, seq_len, shift, scale, g, w, *, per_row, out_dtype, out_scale=1.0):
    n_rows, d = x.shape
    n_out = w.shape[1] // d
    tm = _pick_tm(seq_len, n_rows, per_row, 512)
    x_spec, v_spec = _row_specs(n_rows, seq_len, d, tm, per_row)
    return pl.pallas_call(
        functools.partial(_proj_kernel, out_scale=out_scale),
        grid=(n_rows // tm,),
        in_specs=[x_spec, v_spec, v_spec, pl.BlockSpec((1, d), lambda i: (0, 0)),
                  _resident((d, n_out * d), lambda i: (0, 0))],
        out_specs=[x_spec] * n_out,
        out_shape=[jax.ShapeDtypeStruct((n_rows, d), out_dtype)] * n_out,
        compiler_params=_params(("arbitrary",)),
    )(x, shift, scale, g.reshape(1, d), w)


def _proj_t_kernel(x_ref, sh_ref, sc_ref, g_ref, wt_ref, *o_refs):
    d = x_ref.shape[-1]
    h = _mod_norm(x_ref[...], g_ref[...], sh_ref[...], sc_ref[...]).astype(BF16)
    for k, o_ref in enumerate(o_refs):
        o_ref[...] = _dot_nt(wt_ref[k * d:(k + 1) * d, :], h)


def _norm_proj_t(x, n_seq, seq_len, shift, scale, g, wt):
    n_rows, d = x.shape
    n_out = wt.shape[0] // d
    tm = _pick_tm(seq_len, n_rows, False, 512)
    nt = seq_len // tm
    x_spec, v_spec = _row_specs(n_rows, seq_len, d, tm, False)
    o_spec = pl.BlockSpec((None, d, tm), lambda i: (i // nt, 0, i % nt))
    return pl.pallas_call(
        _proj_t_kernel,
        grid=(n_rows // tm,),
        in_specs=[x_spec, v_spec, v_spec, pl.BlockSpec((1, d), lambda i: (0, 0)),
                  _resident((n_out * d, d), lambda i: (0, 0))],
        out_specs=[o_spec] * n_out,
        out_shape=[jax.ShapeDtypeStruct((n_seq, d, seq_len), F32)] * n_out,
        compiler_params=_params(("arbitrary",)),
    )(x, shift, scale, g.reshape(1, d), wt)


def _log_beta_pair(z):
    sp = jnp.log(1.0 + jnp.exp(-jnp.abs(z)))
    return jnp.minimum(z, 0.0) - sp, -jnp.maximum(z, 0.0) - sp


def _suffix_sum_lanes(l1m, tri):
    hi, lo = _split_bf16(l1m)
    return _dot(hi, tri) + _dot(lo, tri)


def _prompt_attn_kernel(bias_ref, q_ref, kt_ref, vt_ref, o_ref, ktb_ref, vtm_ref, *, tq, nq, hd):
    grp = pl.program_id(1)
    n_hh = MXU_TILE // hd

    ktb_ref[...] = kt_ref[...].astype(BF16)
    vt = vt_ref[...]
    ch_head = lax.broadcasted_iota(jnp.int32, vt.shape, 0) // hd
    for hh in range(n_hh):
        vtm_ref[hh] = jnp.where(ch_head == hh, vt, 0.0).astype(BF16)

    lane_head = lax.broadcasted_iota(jnp.int32, (tq, MXU_TILE), 1) // hd
    r_idx = lax.broadcasted_iota(jnp.int32, (tq, tq), 0)
    c_idx = lax.broadcasted_iota(jnp.int32, (tq, tq), 1)
    tri = (r_idx > c_idx).astype(BF16)
    valid = c_idx < r_idx
    biases = [bias_ref[grp * n_hh + hh] for hh in range(n_hh)]
    heads = range(n_hh)

    def query_block(i, _):
        r0 = pl.multiple_of(i * tq, tq)
        q2 = q_ref[pl.ds(r0, tq), :]
        qns = [jnp.where(lane_head == hh, -q2, jnp.zeros_like(q2)) for hh in heads]
        carry = (tuple(jnp.zeros((tq, 1), F32) for _ in heads), jnp.zeros((tq, MXU_TILE), F32))
        carry = block(i, qns, carry, True)
        carry = lax.fori_loop(0, i, lambda jj, cr: block(i - 1 - jj, qns, cr, False), carry)
        o_ref[pl.ds(r0, tq), :] = carry[1].astype(o_ref.dtype)
        return 0

    def block(j, qns, carry, masked):
        c_sums, acc = carry
        c0 = pl.multiple_of(j * tq, tq)
        kt = ktb_ref[:, pl.ds(c0, tq)]
        groups = [range(h0, min(h0 + HEADS_PER_STAGE, n_hh)) for h0 in range(0, n_hh, HEADS_PER_STAGE)]
        zns, l1bs, log_bs, firsts, sufs, ws, new_sums = {}, {}, {}, {}, {}, {}, {}

        def scores(hs):
            for hh in hs:
                zns[hh] = _dot(qns[hh], kt) - biases[hh]

        def log_terms(hs):
            for hh in hs:
                neg_abs = pltpu.bitcast(pltpu.bitcast(zns[hh], jnp.uint32) | jnp.uint32(0x80000000), F32)
                l1m = jnp.minimum(zns[hh], 0.0) - jnp.log(1.0 + jnp.exp(neg_abs))
                if masked:
                    l1m = jnp.where(valid, l1m, 0.0)
                log_bs[hh] = l1m - zns[hh]
                firsts[hh] = l1m[:, 0:1]
                l1bs[hh] = l1m.astype(BF16)

        def suffix_sums(hs):
            for hh in hs:
                sufs[hh] = _dot(l1bs[hh], tri)

        def weights(hs):
            for hh in hs:
                w = jnp.exp(log_bs[hh] + sufs[hh] + c_sums[hh])
                ws[hh] = (jnp.where(valid, w, 0.0) if masked else w).astype(BF16)
                new_sums[hh] = c_sums[hh] + (sufs[hh][:, 0:1] + firsts[hh])

        for hs in groups:
            scores(hs)
            log_terms(hs)
            suffix_sums(hs)
            weights(hs)
            for hh in hs:
                acc = acc + _dot_nt(ws[hh], vtm_ref[hh, :, pl.ds(c0, tq)])
        return tuple(new_sums[hh] for hh in heads), acc

    lax.fori_loop(0, nq, query_block, 0)


def _prompt_attention(q, kt, vt, bias, n_seq, seq_len, hd):
    n_rows, d = q.shape
    tq = min(256, seq_len)
    nq = seq_len // tq
    n_grp = d // MXU_TILE
    kv_spec = pl.BlockSpec((None, MXU_TILE, seq_len), lambda b, p: (b, p, 0))
    q_spec = pl.BlockSpec((seq_len, MXU_TILE), lambda b, p: (b, p))
    return pl.pallas_call(
        functools.partial(_prompt_attn_kernel, tq=tq, nq=nq, hd=hd),
        grid=(n_seq, n_grp),
        in_specs=[pl.BlockSpec(memory_space=pltpu.SMEM), q_spec, kv_spec, kv_spec],
        out_specs=q_spec,
        out_shape=jax.ShapeDtypeStruct((n_rows, d), BF16),
        scratch_shapes=[pltpu.VMEM((MXU_TILE, seq_len), BF16),
                        pltpu.VMEM((MXU_TILE // hd, MXU_TILE, seq_len), BF16)],
        compiler_params=_params(("arbitrary", "arbitrary")),
    )(bias, q, kt, vt)


def _sample_attn_kernel(pt_ref, qb_ref, bias_ref, knt_ref, vnt_ref, *rest, n_pp, page, n_q, hd):
    k_refs = rest[:n_pp]
    v_refs = rest[n_pp:2 * n_pp]
    o_ref, acc_ref, c_ref = rest[2 * n_pp:]
    j = pl.program_id(1)
    n_tile, rows_t, _ = qb_ref.shape
    n_col = n_tile * rows_t
    bias = bias_ref[...]
    r_idx = lax.broadcasted_iota(jnp.int32, (page, page), 0)
    c_idx = lax.broadcasted_iota(jnp.int32, (page, page), 1)
    tri = (r_idx > c_idx).astype(BF16)

    def scores(kt_ref, valid):
        zs = [_dot(qb_ref[t], kt_ref[t * MXU_TILE:(t + 1) * MXU_TILE, :].astype(BF16)) for t in range(n_tile)]
        z = jnp.concatenate(zs, axis=0) + bias
        log_b, l1m = _log_beta_pair(z)
        if valid is not None:
            l1m = jnp.where(valid, l1m, 0.0)
        suf = _suffix_sum_lanes(l1m, tri)
        return log_b, suf, suf[:, 0:1] + l1m[:, 0:1]

    def weighted_values(log_b, suf, c_sum, vt_ref, valid):
        w = jnp.exp(log_b + suf + c_sum)
        if valid is not None:
            w = jnp.where(valid, w, 0.0)
        wb = w.astype(BF16)
        outs = [_dot_nt(wb[t * rows_t:(t + 1) * rows_t, :], vt_ref[t * MXU_TILE:(t + 1) * MXU_TILE, :].astype(BF16))
                for t in range(n_tile)]
        return jnp.concatenate(outs, axis=0)

    @pl.when(j == 0)
    def _():
        key_i = lax.broadcasted_iota(jnp.int32, (n_col, page), 1)
        qry_i = lax.broadcasted_iota(jnp.int32, (n_col, page), 0) % n_q
        valid = key_i < qry_i
        log_b, suf, tot = scores(knt_ref, valid)
        acc_ref[...] = weighted_values(log_b, suf, jnp.zeros((n_col, 1), F32), vnt_ref, valid)
        c_ref[...] = tot

    @pl.when(j > 0)
    def _():
        parts = [scores(k_refs[pp], None) for pp in range(n_pp)]
        c_sum = c_ref[...]
        acc = acc_ref[...]
        for pp in range(n_pp):
            log_b, suf, tot = parts[pp]
            acc = acc + weighted_values(log_b, suf, c_sum, v_refs[pp], None)
            c_sum = c_sum + tot
        acc_ref[...] = acc
        c_ref[...] = c_sum

    @pl.when(j == pl.num_programs(1) - 1)
    def _():
        row_h = lax.broadcasted_iota(jnp.int32, (rows_t, MXU_TILE), 0) // n_q
        col_h = lax.broadcasted_iota(jnp.int32, (rows_t, MXU_TILE), 1) // hd
        outs = []
        for t in range(n_tile):
            picked = jnp.where(row_h == col_h, acc_ref[t * rows_t:(t + 1) * rows_t, :], 0.0)
            out_t = picked[0:n_q, :]
            for g in range(1, rows_t // n_q):
                out_t = out_t + picked[g * n_q:(g + 1) * n_q, :]
            outs.append(out_t)
        o_ref[...] = jnp.concatenate(outs, axis=1).astype(o_ref.dtype)


def _sample_attention(q, k_new, v_new, cache_kt, cache_vt, page_table, bias, n_q, hd):
    n_rows, d = q.shape
    n_seq = n_rows // n_q
    n_pool, _, page = cache_kt.shape
    n_pages = page_table.shape[1]
    n_pp = max(p for p in (16, 8, 4, 2, 1) if n_pages % p == 0)
    n_steps = n_pages // n_pp
    n_tile = d // MXU_TILE
    hpt = MXU_TILE // hd
    rows_t = hpt * n_q
    n_col = n_tile * rows_t
    q5 = q.reshape(n_seq, n_q, n_tile, hpt, hd).transpose(0, 2, 3, 1, 4)
    qb = q5[:, :, :, :, None, :] * jnp.eye(hpt, dtype=q.dtype)[None, None, :, None, :, None]
    qb = qb.reshape(n_seq, n_tile, rows_t, MXU_TILE)
    bias_rows = jnp.repeat(bias.astype(F32), n_q).reshape(n_col, 1)

    def new_page(a):
        a = jnp.pad(a.reshape(n_seq, n_q, d), ((0, 0), (0, page - n_q), (0, 0)))
        return a.transpose(0, 2, 1)

    seq3 = lambda r, c: pl.BlockSpec((None, r, c), lambda b, j, pt: (b, 0, 0))

    def page_spec(pp):
        def index_map(b, j, pt):
            group = jnp.maximum(j, 1) - 1
            return (pt[b, n_pages - 1 - group * n_pp - pp], 0, 0)
        return pl.BlockSpec((None, d, page), index_map)

    grid_spec = pltpu.PrefetchScalarGridSpec(
        num_scalar_prefetch=1,
        grid=(n_seq, n_steps + 1),
        in_specs=[pl.BlockSpec((None, n_tile, rows_t, MXU_TILE), lambda b, j, pt: (b, 0, 0, 0)),
                  pl.BlockSpec((n_col, 1), lambda b, j, pt: (0, 0)),
                  seq3(d, page), seq3(d, page)]
                 + [page_spec(pp) for pp in range(n_pp)] * 2,
        out_specs=seq3(n_q, d),
        scratch_shapes=[pltpu.VMEM((n_col, MXU_TILE), F32), pltpu.VMEM((n_col, 1), F32)],
    )
    out = pl.pallas_call(
        functools.partial(_sample_attn_kernel, n_pp=n_pp, page=page, n_q=n_q, hd=hd),
        grid_spec=grid_spec,
        out_shape=jax.ShapeDtypeStruct((n_seq, n_q, d), BF16),
        compiler_params=_params(("arbitrary", "arbitrary")),
    )(page_table, qb, bias_rows, new_page(k_new), new_page(v_new), *([cache_kt] * n_pp), *([cache_vt] * n_pp))
    return out.reshape(n_rows, d)


def _trunk(x3, mods, kv_mod, per_row, h0_re, h0_im, past, w):
    n_seq, t_len, d = x3.shape
    n_layers = len(mods)
    n_a = n_layers // 2
    hd = d // w["sb_bias"].shape[1]
    x = x3.reshape(n_seq * t_len, d)
    new_re, new_im = [], []
    k_new = v_new = None
    for l in range(n_layers):
        m = mods[l]
        g = w["norm_g"][l]
        if l == n_a:
            if past is None:
                k_new, v_new = _norm_proj_t(x, n_seq, t_len, kv_mod[0], kv_mod[1], w["kv_norm_g"], w["w_kv_t"])
            else:
                k_new, v_new = _norm_proj(x, t_len, kv_mod[0], kv_mod[1], w["kv_norm_g"], w["w_kv"],
                                          per_row=per_row, out_dtype=F32)
        ffn1 = (w["ffn_w_gate"], w["ffn_w_up"], w["ffn_w_down"], (l, 0))
        ffn2 = (w["ffn_w_gate"], w["ffn_w_up"], w["ffn_w_down"], (l, 1))
        if l < n_a:
            x, u = _ffn(x, t_len, m[0], m[1], m[2], g[0], *ffn1, per_row=per_row, post="mod",
                        post_args=(g[1], m[3], m[4]))
            y, s_re, s_im = _s5_scan(u, n_seq, w["s5_tables"][l], w["ssm_d"][l], h0_re[l], h0_im[l])
            mixer = dict(pre="glu", pre_args=(y, m[5], w["glu_w"][l], w["glu_b"][l]))
            new_re.append(s_re)
            new_im.append(s_im)
        else:
            jb = l - n_a
            x, q = _ffn(x, t_len, m[0], m[1], m[2], g[0], *ffn1, per_row=per_row, post="proj",
                        post_args=(g[1], m[3], m[4], w["w_q"][jb]), proj_scale=hd ** -0.5)
            if past is None:
                o = _prompt_attention(q, k_new, v_new, w["sb_bias"][jb], n_seq, t_len, hd)
            else:
                o = _sample_attention(q, k_new, v_new, past[0], past[1], past[2], w["sb_bias"][jb], t_len, hd)
            mixer = dict(pre="oproj", pre_args=(o, m[5], w["w_o"][jb]))
        if l == n_layers - 1:
            (x,) = _ffn(x, t_len, m[6], m[7], m[8], g[2], *ffn2, per_row=per_row, post="plain",
                        post_args=(w["norm_f"],), emit_x=False, **mixer)
        else:
            (x,) = _ffn(x, t_len, m[6], m[7], m[8], g[2], *ffn2, per_row=per_row, **mixer)
    return x, k_new, v_new, new_re, new_im


def kernel(x_prompt, x_sample, cache_k, cache_v, state_ssm_re, state_ssm_im, page_table,
           c_prompt, c_sample, ada_w, ada_b, norm_g, ffn_w_gate, ffn_w_up, ffn_w_down,
           ssm_a_re, ssm_a_im, ssm_log_dt, ssm_b_re, ssm_b_im, ssm_c_re, ssm_c_im, ssm_d,
           glu_w, glu_b, kv_ada_w, kv_ada_b, kv_norm_g, w_kv, w_q, w_o, sb_bias, norm_f):
    n_p, t_p, d = x_prompt.shape
    n_s, t_s, _ = x_sample.shape
    n_layers = ada_w.shape[0]
    n_a = n_layers // 2
    n_heads = sb_bias.shape[1]
    hd = d // n_heads
    n_pool, page = cache_k.shape[:2]

    c_all = jnp.concatenate([c_prompt, c_sample], axis=0)
    mod_all = _ada_vectors(c_all, ada_w, ada_b)
    kv_all = _ada_vectors(c_all, kv_ada_w[None], kv_ada_b[None])[0]

    def split(vecs, n_vec):
        pr = [vecs[:n_p, k * d:(k + 1) * d].reshape(n_p, 1, d) for k in range(n_vec)]
        sa = [jnp.repeat(vecs[n_p:, k * d:(k + 1) * d], t_s, axis=0) for k in range(n_vec)]
        return pr, sa

    mods_p, mods_s = zip(*[split(mod_all[l], N_MOD) for l in range(n_layers)])
    kv_p, kv_s = split(kv_all, 2)

    w = dict(
        norm_g=norm_g, kv_norm_g=kv_norm_g, norm_f=norm_f, ssm_d=ssm_d, glu_b=glu_b, sb_bias=sb_bias,
        ffn_w_gate=ffn_w_gate.astype(BF16), ffn_w_up=ffn_w_up.astype(BF16), ffn_w_down=ffn_w_down.astype(BF16),
        glu_w=glu_w.astype(BF16), w_kv=w_kv.astype(BF16), w_kv_t=w_kv.T.astype(BF16),
        w_q=w_q.astype(BF16), w_o=w_o.astype(BF16),
        s5_tables=[_s5_tables(ssm_a_re[l], ssm_a_im[l], ssm_log_dt[l], ssm_b_re[l], ssm_b_im[l],
                              ssm_c_re[l], ssm_c_im[l]) for l in range(n_a)],
    )

    zeros = jnp.zeros((n_a, n_p, state_ssm_re.shape[2] * state_ssm_re.shape[3]), F32)
    y_p, k_p, v_p, re_p, im_p = _trunk(x_prompt, mods_p, kv_p, False, zeros, zeros, None, w)

    h0_re = state_ssm_re.reshape(n_a, n_s, -1)
    h0_im = state_ssm_im.reshape(n_a, n_s, -1)
    to_cm = lambda c: c.transpose(0, 2, 3, 1).reshape(n_pool, d, page)
    past = (to_cm(cache_k), to_cm(cache_v), page_table)
    y_s, k_s, v_s, re_s, im_s = _trunk(x_sample, mods_s, kv_s, True, h0_re, h0_im, past, w)

    from_cm = lambda a: a.reshape(n_p, n_heads, hd, t_p).transpose(0, 3, 1, 2)
    st_shape_p = (n_a, n_p) + state_ssm_re.shape[2:]
    st_shape_s = (n_a, n_s) + state_ssm_re.shape[2:]
    return (y_p.reshape(n_p, t_p, d), y_s.reshape(n_s, t_s, d),
            from_cm(k_p), from_cm(v_p),
            k_s.reshape(n_s, t_s, n_heads, hd), v_s.reshape(n_s, t_s, n_heads, hd),
            jnp.stack(re_p).reshape(st_shape_p), jnp.stack(im_p).reshape(st_shape_p),
            jnp.stack(re_s).reshape(st_shape_s), jnp.stack(im_s).reshape(st_shape_s))
```
